```python
import jax
import jax.numpy as jnp
from jax import lax
import numpy as np

D_MODEL = 2048
BATCH = 2
SEQ = 4096
DEPTH = 4
DEC_BATCH = 8
DEC_SEQ = 4
PAST_LEN = 16384
PAGE_SIZE = 128

N_HEADS = 16
HEAD_DIM = D_MODEL // N_HEADS
D_ATT = N_HEADS * HEAD_DIM
D_CONV = D_MODEL
CONV_W = 3
BLOCK = 256
TOPK_BLOCKS = 3
Q_CHUNK = 16
D_FF = 5632
N_EXPERTS = 8
MOE_TOPK = 2
D_EXPERT = 7168
ROUTE_ROWS = 128
PLE_DIM = 256
N_DENSE = (DEPTH + 1) // 2
N_MOE = DEPTH // 2
RMS_EPS = 1e-6
NEG = -1e30
IN_WIDTHS = (D_ATT, D_ATT, D_ATT, D_CONV, D_CONV, D_CONV, D_MODEL, D_MODEL)
D_IN = sum(IN_WIDTHS)
SPLIT_AT = tuple(sum(IN_WIDTHS[:j + 1]) for j in range(len(IN_WIDTHS) - 1))

kernel_name = 'hybrid_conv_moba_decoder_step'


def _rmsnorm(x, g):
    xf = x.astype(jnp.float32)
    y = xf * lax.rsqrt(jnp.mean(xf * xf, axis=-1, keepdims=True) + RMS_EPS)
    return (y * g.astype(jnp.float32)).astype(x.dtype)


def _swiglu(x, w1, w3, w2):
    return (jax.nn.silu(x @ w1) * (x @ w3)) @ w2


def _moe_swiglu(x, router, w1, w3, w2):
    n = x.shape[0]
    logits = jnp.einsum('nd,de->ne', x, router, preferred_element_type=jnp.float32)
    top_logit, top_e = lax.top_k(logits, MOE_TOPK)
    gates = jax.nn.softmax(top_logit, axis=-1)
    n_slots = n * MOE_TOPK
    flat_e = top_e.reshape(-1)
    order = jnp.argsort(flat_e)
    sorted_e = flat_e[order]
    counts = jnp.zeros((N_EXPERTS,), jnp.int32).at[flat_e].add(1)
    padded = (counts + ROUTE_ROWS - 1) // ROUTE_ROWS * ROUTE_ROWS
    end_pad = jnp.cumsum(padded)
    start_pad = end_pad - padded
    start = jnp.cumsum(counts) - counts
    dest = start_pad[sorted_e] + jnp.arange(n_slots) - start[sorted_e]
    n_blocks = -(-n_slots // ROUTE_ROWS) + N_EXPERTS
    tok = order // MOE_TOPK
    buf = jnp.zeros((n_blocks * ROUTE_ROWS, x.shape[1]), x.dtype).at[dest].set(x[tok])
    block_e = jnp.minimum(jnp.searchsorted(end_pad, jnp.arange(n_blocks) * ROUTE_ROWS, side='right'), N_EXPERTS - 1)

    def expert_block(args):
        xb, e = args
        return (jax.nn.silu(xb @ w1[e]) * (xb @ w3[e])) @ w2[e]

    out = lax.map(expert_block, (buf.reshape(n_blocks, ROUTE_ROWS, -1), block_e)).reshape(n_blocks * ROUTE_ROWS, -1)
    y_slots = out[dest] * gates.reshape(-1)[order][:, None].astype(x.dtype)
    return jnp.zeros_like(x).at[tok].add(y_slots)


def _select_blocks(q, means, n_past):
    nb = means.shape[1]
    sc = jnp.einsum('bqhd,bnhd->bqhn', q.astype(jnp.float32), means)
    sc = jnp.where(jnp.arange(nb) < n_past, sc, NEG)
    if nb < TOPK_BLOCKS:
        sc = jnp.pad(sc, ((0, 0), (0, 0), (0, 0), (0, TOPK_BLOCKS - nb)), constant_values=NEG)
    _, idx = lax.top_k(sc, TOPK_BLOCKS)
    valid = jnp.broadcast_to(jnp.arange(TOPK_BLOCKS) < n_past, idx.shape)
    return jnp.minimum(idx, max(nb - 1, 0)), valid


def _attend(q, k_sel, v_sel, pos_sel, valid_sel, k_own, v_own, pos_own, tq, slopes):
    scale = HEAD_DIM ** -0.5
    s_sel = jnp.einsum('bqhd,bqhkd->bqhk', q, k_sel, preferred_element_type=jnp.float32) * scale
    s_sel = s_sel - slopes[:, None] * (tq[:, None, None] - pos_sel)
    s_sel = jnp.where(valid_sel, s_sel, NEG)
    s_own = jnp.einsum('bqhd,bhkd->bqhk', q, k_own, preferred_element_type=jnp.float32) * scale
    dist = (tq[:, None] - pos_own[None, :])[:, None, :]
    s_own = jnp.where(dist >= 0, s_own - slopes[:, None] * dist, NEG)
    ks = s_sel.shape[-1]
    probs = jax.nn.softmax(jnp.concatenate([s_sel, s_own], axis=-1), axis=-1).astype(v_own.dtype)
    return (jnp.einsum('bqhk,bqhkd->bqhd', probs[..., :ks], v_sel)
            + jnp.einsum('bqhk,bhkd->bqhd', probs[..., ks:], v_own))


def _moba_prompt(q, k, v, slopes):
    b, s, h, hd = q.shape
    nb = -(-s // BLOCK)
    s_pad = nb * BLOCK
    pad = ((0, 0), (0, s_pad - s), (0, 0), (0, 0))
    q, k, v = (jnp.pad(a, pad) for a in (q, k, v))
    k_blk = k.reshape(b, nb, BLOCK, h, hd)
    means = jnp.mean(k_blk.astype(jnp.float32), axis=2)
    kb = k_blk.transpose(0, 3, 1, 2, 4)
    vb = v.reshape(b, nb, BLOCK, h, hd).transpose(0, 3, 1, 2, 4)
    n_chunks = s_pad // Q_CHUNK
    q_chunks = q.reshape(b, n_chunks, Q_CHUNK, h, hd).transpose(1, 0, 2, 3, 4)
    b_ix = jnp.arange(b)[:, None, None, None]
    h_ix = jnp.arange(h)[None, None, :, None]
    offs = jnp.arange(BLOCK)
    ks = TOPK_BLOCKS * BLOCK

    def chunk(args):
        q_c, c = args
        t0 = c * Q_CHUNK
        tq = t0 + jnp.arange(Q_CHUNK)
        ob = t0 // BLOCK
        idx, valid = _select_blocks(q_c, means, ob)
        k_sel = kb[b_ix, h_ix, idx].reshape(b, Q_CHUNK, h, ks, hd)
        v_sel = vb[b_ix, h_ix, idx].reshape(b, Q_CHUNK, h, ks, hd)
        pos_sel = (idx[..., None] * BLOCK + offs).reshape(b, Q_CHUNK, h, ks)
        k_own = lax.dynamic_index_in_dim(kb, ob, axis=2, keepdims=False)
        v_own = lax.dynamic_index_in_dim(vb, ob, axis=2, keepdims=False)
        return _attend(q_c, k_sel, v_sel, pos_sel, jnp.repeat(valid, BLOCK, axis=-1),
                       k_own, v_own, ob * BLOCK + offs, tq, slopes)

    out = lax.map(chunk, (q_chunks, jnp.arange(n_chunks)))
    return out.transpose(1, 0, 2, 3, 4).reshape(b, s_pad, h * hd)[:, :s]


def _moba_sample(q, k_new, v_new, cache_k, cache_v, layer, page_table, slopes):
    b, t, h, hd = q.shape
    n_pages = page_table.shape[1]
    past_len = n_pages * PAGE_SIZE
    ppb = BLOCK // PAGE_SIZE
    nb = past_len // BLOCK
    own_start = nb * BLOCK
    n_tail = (past_len - own_start) // PAGE_SIZE
    tq = past_len + jnp.arange(t)
    block_pages = page_table[:, :nb * ppb].reshape(b, nb, ppb)
    means = jnp.mean(cache_k[layer, block_pages].astype(jnp.float32), axis=(2, 4))
    idx, valid = _select_blocks(q, means, nb)
    cols = jnp.minimum(idx[..., None] * ppb + jnp.arange(ppb), n_pages - 1)
    phys = page_table[jnp.arange(b)[:, None, None, None, None], cols]
    h_ix = jnp.arange(h)[None, None, :, None, None]
    ks = TOPK_BLOCKS * BLOCK
    k_sel = cache_k[layer, phys, h_ix].reshape(b, t, h, ks, hd)
    v_sel = cache_v[layer, phys, h_ix].reshape(b, t, h, ks, hd)
    pos_sel = (idx[..., None] * BLOCK + jnp.arange(BLOCK)).reshape(b, t, h, ks)
    tail = page_table[:, nb * ppb: nb * ppb + n_tail]

    def own(cache, new):
        rows = cache[layer, tail].transpose(0, 2, 1, 3, 4).reshape(b, h, n_tail * PAGE_SIZE, hd)
        return jnp.concatenate([rows, new.transpose(0, 2, 1, 3)], axis=2)

    k_own, v_own = own(cache_k, k_new), own(cache_v, v_new)
    pos_own = own_start + jnp.arange(n_tail * PAGE_SIZE + t)
    out = _attend(q, k_sel, v_sel, pos_sel, jnp.repeat(valid, BLOCK, axis=-1),
                  k_own, v_own, pos_own, tq, slopes)
    return out.reshape(b, t, h * hd)


def _to_pages(k):
    b, s, h, hd = k.shape
    return k.reshape(b, s // PAGE_SIZE, PAGE_SIZE, h, hd).transpose(0, 1, 3, 2, 4)


def setup_inputs(seed: int = 0) -> dict:
    key = jax.random.key(seed)
    keys = jax.random.split(key, 32)
    f32 = jnp.float32
    n_pages = PAST_LEN // PAGE_SIZE
    n_pool = (DEC_BATCH * n_pages * 5) // 4

    def normal(i, shape, scale=1.0):
        return jax.random.normal(keys[i], shape, f32) * scale

    def gain(i, shape):
        return 1.0 + 0.1 * jax.random.normal(keys[i], shape, f32)

    page_table = jax.random.permutation(keys[5], n_pool)[:DEC_BATCH * n_pages].reshape(DEC_BATCH, n_pages).astype(jnp.int32)
    return {
        'x_prompt': normal(0, (BATCH, SEQ, D_MODEL)),
        'x_sample': normal(1, (DEC_BATCH, DEC_SEQ, D_MODEL)),
        'cache_k': normal(2, (DEPTH, n_pool, N_HEADS, PAGE_SIZE, HEAD_DIM)),
        'cache_v': normal(3, (DEPTH, n_pool, N_HEADS, PAGE_SIZE, HEAD_DIM)),
        'state_conv': normal(4, (DEPTH, DEC_BATCH, CONV_W - 1, D_CONV)),
        'page_table': page_table,
        'p_prompt': normal(6, (DEPTH, BATCH, SEQ, PLE_DIM)),
        'p_sample': normal(7, (DEPTH, DEC_BATCH, DEC_SEQ, PLE_DIM)),
        'norm_mix': gain(8, (DEPTH, D_MODEL)),
        'w_in': normal(9, (DEPTH, D_MODEL, D_IN), D_MODEL ** -0.5),
        'conv_w': normal(10, (DEPTH, CONV_W, D_CONV), CONV_W ** -0.5),
        'w_out': normal(11, (DEPTH, D_MODEL, D_MODEL), D_MODEL ** -0.5),
        'norm_ffn': gain(12, (DEPTH, D_MODEL)),
        'dense_w1': normal(13, (N_DENSE, D_MODEL, D_FF), D_MODEL ** -0.5),
        'dense_w3': normal(14, (N_DENSE, D_MODEL, D_FF), D_MODEL ** -0.5),
        'dense_w2': normal(15, (N_DENSE, D_FF, D_MODEL), D_FF ** -0.5),
        'moe_router': normal(16, (N_MOE, D_MODEL, N_EXPERTS), D_MODEL ** -0.5),
        'moe_w1': normal(17, (N_MOE, N_EXPERTS, D_MODEL, D_EXPERT), D_MODEL ** -0.5),
        'moe_w3': normal(18, (N_MOE, N_EXPERTS, D_MODEL, D_EXPERT), D_MODEL ** -0.5),
        'moe_w2': normal(19, (N_MOE, N_EXPERTS, D_EXPERT, D_MODEL), D_EXPERT ** -0.5),
        'norm_ple': gain(20, (DEPTH, D_MODEL)),
        'w_ple_gate': normal(21, (DEPTH, D_MODEL, D_MODEL), D_MODEL ** -0.5),
        'w_ple_proj': normal(22, (DEPTH, PLE_DIM, D_MODEL), PLE_DIM ** -0.5),
        'norm_final': gain(23, (D_MODEL,)),
    }


def reference(x_prompt, x_sample, cache_k, cache_v, state_conv, page_table, p_prompt, p_sample,
              norm_mix, w_in, conv_w, w_out, norm_ffn, dense_w1, dense_w3, dense_w2,
              moe_router, moe_w1, moe_w3, moe_w2, norm_ple, w_ple_gate, w_ple_proj, norm_final):
    slopes = jnp.exp2(-8.0 * jnp.arange(1, N_HEADS + 1, dtype=jnp.float32) / N_HEADS)

    def layer(x, p_i, i, attend, conv_past):
        b, t, _ = x.shape
        xn = _rmsnorm(x, norm_mix[i])
        q, k, v, gate_b, gate_c, xc, g_att, g_conv = jnp.split(xn @ w_in[i], SPLIT_AT, axis=-1)
        heads = lambda a: a.reshape(b, t, N_HEADS, HEAD_DIM)
        att, k_rows, v_rows = attend(heads(q), heads(k), heads(v), i)
        u_ext = jnp.concatenate([conv_past, gate_c * xc], axis=1)
        conv = conv_w[i, 0] * u_ext[:, 0:t]
        for j in range(1, CONV_W):
            conv = conv + conv_w[i, j] * u_ext[:, j:j + t]
        mix = jax.nn.sigmoid(g_att) * att + jax.nn.sigmoid(g_conv) * (gate_b * conv)
        x = x + mix @ w_out[i]
        xn = _rmsnorm(x, norm_ffn[i])
        if i % 2 == 0:
            f = _swiglu(xn, dense_w1[i // 2], dense_w3[i // 2], dense_w2[i // 2])
        else:
            m = i // 2
            f = _moe_swiglu(xn.reshape(b * t, D_MODEL), moe_router[m], moe_w1[m], moe_w3[m],
                            moe_w2[m]).reshape(b, t, D_MODEL)
        x = x + f
        g = jax.nn.sigmoid(_rmsnorm(x, norm_ple[i]) @ w_ple_gate[i])
        x = x + g * (p_i @ w_ple_proj[i])
        return x, k_rows, v_rows, u_ext[:, t:]

    def attend_prompt(q, k, v, i):
        return _moba_prompt(q, k, v, slopes), _to_pages(k), _to_pages(v)

    def attend_sample(q, k, v, i):
        out = _moba_sample(q, k, v, cache_k, cache_v, i, page_table, slopes)
        return out, k.transpose(0, 2, 1, 3), v.transpose(0, 2, 1, 3)

    xp, xs = x_prompt, x_sample
    conv_zero = jnp.zeros((x_prompt.shape[0], CONV_W - 1, D_CONV), x_prompt.dtype)
    kp, vp, cp, ksm, vsm, csm = [], [], [], [], [], []
    for i in range(DEPTH):
        xp, kr, vr, cr = layer(xp, p_prompt[i], i, attend_prompt, conv_zero)
        kp.append(kr); vp.append(vr); cp.append(cr)
        xs, kr, vr, cr = layer(xs, p_sample[i], i, attend_sample, state_conv[i])
        ksm.append(kr); vsm.append(vr); csm.append(cr)
    y_prompt = _rmsnorm(xp, norm_final)
    y_sample = _rmsnorm(xs, norm_final)
    return (y_prompt, y_sample, jnp.stack(kp), jnp.stack(vp), jnp.stack(cp),
            jnp.stack(ksm), jnp.stack(vsm), jnp.stack(csm))
```

```python
import functools

import jax
import jax.numpy as jnp
from jax import lax
from jax.experimental import pallas as pl
from jax.experimental.pallas import tpu as pltpu

F32 = jnp.float32
BF16 = jnp.bfloat16
I32 = jnp.int32

RMS_EPS = 1e-6
NEG = -1e30
BLOCK = 256
TOPK_BLOCKS = 3
MOE_TOPK = 2
CONV_W = 3
N_IN_FIELDS = 8

LANE = 128
SUBLANE = 8
ROW_TILE = 256
VMEM_LIMIT = 56 * 1024 * 1024


def _cparams(n_axes, vmem=VMEM_LIMIT):
    return pltpu.CompilerParams(dimension_semantics=("arbitrary",) * n_axes, vmem_limit_bytes=vmem)


def _row_tile(rows, max_tile):
    n = rows // ROW_TILE
    best = 1
    for k in range(1, n + 1):
        if n % k == 0 and k * ROW_TILE <= max_tile:
            best = k
    return best * ROW_TILE


def _col_tile(cols, max_tile):
    best = LANE
    for t in range(LANE, min(cols, max_tile) + 1, LANE):
        if cols % t == 0:
            best = t
    return best


def _rmsnorm(x, g):
    ms = jnp.mean(x * x, axis=-1, keepdims=True)
    return x * lax.rsqrt(ms + RMS_EPS) * g


def _dot_nt(a, b):
    return lax.dot_general(a, b, (((1,), (1,)), ((), ())), preferred_element_type=F32)


def _split_bf16(x):
    hi = x.astype(BF16)
    lo = (x - hi.astype(F32)).astype(BF16)
    return hi, lo


def _top_picks(cand, col, n, sentinel):
    picks = []
    for _ in range(n):
        m = jnp.max(cand, axis=-1, keepdims=True)
        first = jnp.min(jnp.where(cand == m, col, sentinel), axis=-1, keepdims=True)
        picks.append(first)
        cand = jnp.where(col == first, -jnp.inf, cand)
    return picks


def _inproj_kernel(x_ref, g_ref, w_ref, o_ref, xn_ref):
    @pl.when(pl.program_id(1) == 0)
    def _():
        xn_ref[...] = _rmsnorm(x_ref[...], g_ref[...]).astype(BF16)

    o_ref[...] = jnp.dot(xn_ref[...], w_ref[...], preferred_element_type=F32)


def _inproj(x, g, w, layer):
    rows, d = x.shape
    n = w.shape[-1]
    tm = _row_tile(rows, 768)
    tn = _col_tile(n, 1024)
    return pl.pallas_call(
        _inproj_kernel,
        grid=(rows // tm, n // tn),
        in_specs=[
            pl.BlockSpec((tm, d), lambda i, j: (i, 0)),
            pl.BlockSpec((None, 1, d), lambda i, j: (layer, 0, 0)),
            pl.BlockSpec((None, d, tn), lambda i, j: (layer, 0, j)),
        ],
        out_specs=pl.BlockSpec((tm, tn), lambda i, j: (i, j)),
        out_shape=jax.ShapeDtypeStruct((rows, n), F32),
        scratch_shapes=[pltpu.VMEM((tm, d), BF16)],
        compiler_params=_cparams(2),
        name="inproj",
    )(x, g, w)


def _moba_prompt_kernel(slopes_ref, q_ref, k_ref, v_ref, o_ref, kb_ref, vb_ref, mt_ref, arel_ref, *, nb, hd, nbp):
    slope = slopes_ref[pl.program_id(1)]
    scale = hd ** -0.5
    k = k_ref[...]
    kb_ref[...] = k.astype(BF16)
    vb_ref[...] = v_ref[...].astype(BF16)
    means = jnp.mean(k.reshape(nb, BLOCK, hd), axis=1)
    m_hi, m_lo = _split_bf16(means)
    mt_ref[...] = jnp.zeros_like(mt_ref)
    mt_ref[0:nb, :] = jnp.concatenate([m_hi, m_hi, m_lo], axis=1)
    r_io = lax.broadcasted_iota(I32, (BLOCK, BLOCK), 0)
    c_io = lax.broadcasted_iota(I32, (BLOCK, BLOCK), 1)
    arel_ref[...] = slope * (r_io - c_io).astype(F32)
    col = lax.broadcasted_iota(I32, (BLOCK, nbp), 1)

    def qblock(i, carry):
        row0 = pl.multiple_of(i * BLOCK, BLOCK)
        qs = q_ref[pl.ds(row0, BLOCK), :]
        q_hi, q_lo = _split_bf16(qs)
        sc = _dot_nt(jnp.concatenate([q_hi, q_lo, q_hi], axis=1), mt_ref[...])
        picks = _top_picks(jnp.where(col < i, sc, NEG), col, TOPK_BLOCKS, nbp)
        picks = [jnp.where(r < i, p, -1) for r, p in enumerate(picks)]
        qb = (qs * scale).astype(BF16)

        s = _dot_nt(qb, kb_ref[pl.ds(row0, BLOCK), :]) - arel_ref[...]
        s = jnp.where(r_io >= c_io, s, NEG)
        m = jnp.max(s, axis=-1, keepdims=True)
        p = jnp.exp(s - m)
        l = jnp.sum(p, axis=-1, keepdims=True)
        acc = jnp.dot(p.astype(BF16), vb_ref[pl.ds(row0, BLOCK), :], preferred_element_type=F32)

        def kblock(j, st):
            m, l, acc = st
            col0 = pl.multiple_of(j * BLOCK, BLOCK)
            hit = (picks[0] == j) | (picks[1] == j) | (picks[2] == j)
            pen = jnp.where(hit, 0.0, NEG) - slope * jnp.asarray((i - j) * BLOCK, F32)
            s = _dot_nt(qb, kb_ref[pl.ds(col0, BLOCK), :]) - arel_ref[...] + pen
            m_new = jnp.maximum(m, jnp.max(s, axis=-1, keepdims=True))
            alpha = jnp.exp(m - m_new)
            p = jnp.exp(s - m_new)
            l = alpha * l + jnp.sum(p, axis=-1, keepdims=True)
            acc = alpha * acc + jnp.dot(p.astype(BF16), vb_ref[pl.ds(col0, BLOCK), :], preferred_element_type=F32)
            return m_new, l, acc

        m, l, acc = lax.fori_loop(0, i, kblock, (m, l, acc))
        o_ref[pl.ds(row0, BLOCK), :] = acc / l
        return carry

    lax.fori_loop(0, nb, qblock, 0)


def _moba_prompt(h, slopes, b, s, n_heads, hd):
    assert s % BLOCK == 0 and TOPK_BLOCKS == 3
    nb = s // BLOCK
    nbp = -(-nb // LANE) * LANE
    kern = functools.partial(_moba_prompt_kernel, nb=nb, hd=hd, nbp=nbp)
    return pl.pallas_call(
        kern,
        grid=(b, n_heads),
        in_specs=[
            pl.BlockSpec(memory_space=pltpu.SMEM),
            pl.BlockSpec((s, hd), lambda bi, hi: (bi, hi)),
            pl.BlockSpec((s, hd), lambda bi, hi: (bi, n_heads + hi)),
            pl.BlockSpec((s, hd), lambda bi, hi: (bi, 2 * n_heads + hi)),
        ],
        out_specs=pl.BlockSpec((s, hd), lambda bi, hi: (bi, hi)),
        out_shape=jax.ShapeDtypeStruct((b * s, n_heads * hd), F32),
        scratch_shapes=[
            pltpu.VMEM((s, hd), BF16),
            pltpu.VMEM((s, hd), BF16),
            pltpu.VMEM((nbp, 3 * hd), BF16),
            pltpu.VMEM((BLOCK, BLOCK), F32),
        ],
        compiler_params=_cparams(2),
        name="moba_prompt",
    )(slopes, h, h, h)


def _block_means_kernel(pt_ref, *refs, ppb, page):
    del pt_ref
    o_ref = refs[ppb]
    tot = jnp.sum(refs[0][0, 0], axis=1)
    for r in refs[1:ppb]:
        tot = tot + jnp.sum(r[0, 0], axis=1)
    o_ref[0, 0, 0] = tot * (1.0 / (ppb * page))


def _block_means(cache_k, page_table):
    depth, _, n_heads, page, hd = cache_k.shape
    db, n_pages = page_table.shape
    ppb = BLOCK // page
    nbk = n_pages // ppb
    pt = page_table.reshape(-1)

    def page_map(jp, l, b, n, pt_ref):
        return (l, pt_ref[b * n_pages + n * ppb + jp], 0, 0, 0)

    return pl.pallas_call(
        functools.partial(_block_means_kernel, ppb=ppb, page=page),
        grid_spec=pltpu.PrefetchScalarGridSpec(
            num_scalar_prefetch=1,
            grid=(depth, db, nbk),
            in_specs=[pl.BlockSpec((1, 1, n_heads, page, hd), functools.partial(page_map, jp)) for jp in range(ppb)],
            out_specs=pl.BlockSpec((1, 1, 1, n_heads, hd), lambda l, b, n, pt_ref: (l, b, n, 0, 0)),
        ),
        out_shape=jax.ShapeDtypeStruct((depth, db, nbk, n_heads, hd), F32),
        compiler_params=_cparams(3),
        name="block_means",
    )(pt, *([cache_k] * ppb))


def _sample_select_kernel(q_ref, m_ref, o_ref, *, nbk):
    q = q_ref[0]
    means = m_ref[0, 0]
    sc = lax.dot_general(q, means, (((2,), (2,)), ((0,), (0,))),
                         precision=lax.Precision.HIGHEST, preferred_element_type=F32)
    col = lax.broadcasted_iota(I32, sc.shape, 2)
    picks = _top_picks(sc, col, TOPK_BLOCKS, nbk)
    lane = lax.broadcasted_iota(I32, o_ref.shape[1:], 2)
    o_ref[0] = jnp.where(lane == 0, picks[0], jnp.where(lane == 1, picks[1], picks[2]))


def _sample_select(q_s, means_t, layer):
    db, n_heads, tp, hd = q_s.shape
    nbk = means_t.shape[3]
    assert nbk >= TOPK_BLOCKS
    return pl.pallas_call(
        functools.partial(_sample_select_kernel, nbk=nbk),
        grid=(db,),
        in_specs=[
            pl.BlockSpec((1, n_heads, tp, hd), lambda b: (b, 0, 0, 0)),
            pl.BlockSpec((1, 1, n_heads, nbk, hd), lambda b: (layer, b, 0, 0, 0)),
        ],
        out_specs=pl.BlockSpec((1, n_heads, tp, LANE), lambda b: (b, 0, 0, 0)),
        out_shape=jax.ShapeDtypeStruct((db, n_heads, tp, LANE), I32),
        compiler_params=_cparams(1),
        name="sample_select",
    )(q_s, means_t)


def _sample_attend_kernel(phys_ref, blk_ref, slopes_ref, q_ref, kn_ref, vn_ref, ck_hbm, cv_hbm, o_ref, kbuf, vbuf, sems,
                          *, layer, t_len, n_sel, past_len, hd, page):
    b, hh = pl.program_id(0), pl.program_id(1)
    n_heads = pl.num_programs(1)
    step = b * n_heads + hh
    n_steps = pl.num_programs(0) * n_heads

    def page_copies(st, slot):
        sb, sh = st // n_heads, st % n_heads
        cps = []
        for t in range(t_len):
            for c in range(n_sel):
                pg = phys_ref[((sb * t_len + t) * n_heads + sh) * n_sel + c]
                j = t * n_sel + c
                cps.append(pltpu.make_async_copy(ck_hbm.at[layer, pg, sh], kbuf.at[slot, j], sems.at[0, slot]))
                cps.append(pltpu.make_async_copy(cv_hbm.at[layer, pg, sh], vbuf.at[slot, j], sems.at[1, slot]))
        return cps

    @pl.when(step == 0)
    def _():
        for cp in page_copies(step, 0):
            cp.start()

    @pl.when(step + 1 < n_steps)
    def _():
        for cp in page_copies(step + 1, (step + 1) % 2):
            cp.start()

    slot = step % 2
    for cp in page_copies(step, slot):
        cp.wait()

    slope = slopes_ref[hh]
    scale = hd ** -0.5
    ppb = BLOCK // page
    qb = (q_ref[0, 0] * scale).astype(BF16)
    tp = qb.shape[0]
    s_own_all = _dot_nt(qb, kn_ref[0, 0].astype(BF16))
    vn = vn_ref[0, 0].astype(BF16)
    lane = lax.broadcasted_iota(I32, (1, page), 1)
    tcol = lax.broadcasted_iota(I32, (1, tp), 1)
    outs = []
    for t in range(t_len):
        tq = past_len + t
        s_list = []
        for c in range(n_sel):
            blk = blk_ref[((b * t_len + t) * n_heads + hh) * TOPK_BLOCKS + c // ppb]
            pos = blk * BLOCK + (c % ppb) * page + lane
            s = _dot_nt(qb, kbuf[slot, t * n_sel + c].astype(BF16))[t:t + 1, :]
            s_list.append(s - slope * (tq - pos).astype(F32))
        s_own = s_own_all[t:t + 1, :] - slope * (t - tcol).astype(F32)
        s_own = jnp.where(tcol <= t, s_own, NEG)
        m = jnp.max(s_own, axis=-1, keepdims=True)
        for s in s_list:
            m = jnp.maximum(m, jnp.max(s, axis=-1, keepdims=True))
        p_own = jnp.exp(s_own - m)
        l = jnp.sum(p_own, axis=-1, keepdims=True)
        acc = jnp.dot(jnp.broadcast_to(p_own, (tp, tp)).astype(BF16), vn, preferred_element_type=F32)[0:1, :]
        for c, s in enumerate(s_list):
            p = jnp.exp(s - m)
            l = l + jnp.sum(p, axis=-1, keepdims=True)
            pv = jnp.dot(jnp.broadcast_to(p, (tp, page)).astype(BF16), vbuf[slot, t * n_sel + c].astype(BF16),
                         preferred_element_type=F32)
            acc = acc + pv[0:1, :]
        outs.append(acc / l)
    outs += [jnp.zeros((1, hd), F32)] * (tp - t_len)
    o_ref[0, 0] = jnp.concatenate(outs, axis=0)


def _sample_attend(q_s, k_new, v_new, cache_k, cache_v, phys, blk, slopes, layer, t_len, past_len):
    db, n_heads, tp, hd = q_s.shape
    page = cache_k.shape[3]
    n_sel = TOPK_BLOCKS * (BLOCK // page)
    n_pg = t_len * n_sel
    small = pl.BlockSpec((1, 1, tp, hd), lambda b, hh, phys_ref, blk_ref: (b, hh, 0, 0))
    anyspec = pl.BlockSpec(memory_space=pl.ANY)
    kern = functools.partial(_sample_attend_kernel, layer=layer, t_len=t_len, n_sel=n_sel, past_len=past_len, hd=hd,
                             page=page)
    return pl.pallas_call(
        kern,
        grid_spec=pltpu.PrefetchScalarGridSpec(
            num_scalar_prefetch=2,
            grid=(db, n_heads),
            in_specs=[pl.BlockSpec(memory_space=pltpu.SMEM), small, small, small, anyspec, anyspec],
            out_specs=small,
            scratch_shapes=[pltpu.VMEM((2, n_pg, page, hd), F32), pltpu.VMEM((2, n_pg, page, hd), F32),
                            pltpu.SemaphoreType.DMA((2, 2))],
        ),
        out_shape=jax.ShapeDtypeStruct((db, n_heads, tp, hd), F32),
        compiler_params=_cparams(2),
        name="sample_attend",
    )(phys, blk, slopes, q_s, k_new, v_new, cache_k, cache_v)


def _mix(gb, ga, gcv, att, u0, u1, u2, cw):
    conv = cw[0:1] * u0 + cw[1:2] * u1 + cw[2:3] * u2
    return jax.nn.sigmoid(ga) * att + jax.nn.sigmoid(gcv) * (gb * conv)


def _mix_prompt_kernel(gb_ref, gc_ref, xc_ref, ga_ref, gcv_ref, gch_ref, xch_ref, att_ref, cw_ref, o_ref, us_ref, *, tiles_per_seq):
    i = pl.program_id(0)
    cw = cw_ref[...]
    u = gc_ref[...] * xc_ref[...]
    halo = jnp.where(i % tiles_per_seq == 0, 0.0, gch_ref[...] * xch_ref[...])
    u1 = pltpu.roll(u, 1, 0)
    u2 = pltpu.roll(u, 2, 0)
    o_ref[...] = _mix(gb_ref[...], ga_ref[...], gcv_ref[...], att_ref[...], u2, u1, u, cw).astype(o_ref.dtype)
    row = lax.broadcasted_iota(I32, (SUBLANE, u.shape[1]), 0)
    h1 = pltpu.roll(halo, 1, 0)
    h2 = pltpu.roll(halo, 2, 0)
    f1 = jnp.where(row < 1, h1, u1[0:SUBLANE])
    f2 = jnp.where(row < 2, h2, u2[0:SUBLANE])
    o_ref[0:SUBLANE, :] = _mix(gb_ref[0:SUBLANE, :], ga_ref[0:SUBLANE, :], gcv_ref[0:SUBLANE, :], att_ref[0:SUBLANE, :],
                               f2, f1, u[0:SUBLANE], cw).astype(o_ref.dtype)
    us_ref[...] = u[u.shape[0] - SUBLANE:, :]


def _mix_prompt(h, att, conv_w, layer, b, s, d):
    assert CONV_W == 3
    tm = ROW_TILE
    tps = s // tm
    hpt = tm // SUBLANE

    def field(f):
        return pl.BlockSpec((tm, d), lambda i: (i, f))

    def halo(f):
        return pl.BlockSpec((SUBLANE, d), lambda i: (jnp.maximum(i * hpt - 1, 0), f))

    return pl.pallas_call(
        functools.partial(_mix_prompt_kernel, tiles_per_seq=tps),
        grid=(b * tps,),
        in_specs=[field(3), field(4), field(5), field(6), field(7), halo(4), halo(5),
                  pl.BlockSpec((tm, d), lambda i: (i, 0)),
                  pl.BlockSpec((None, CONV_W, d), lambda i: (layer, 0, 0))],
        out_specs=[pl.BlockSpec((tm, d), lambda i: (i, 0)),
                   pl.BlockSpec((None, SUBLANE, d), lambda i: (i // tps, 0, 0))],
        out_shape=[jax.ShapeDtypeStruct((b * s, d), BF16), jax.ShapeDtypeStruct((b, SUBLANE, d), F32)],
        compiler_params=_cparams(1),
        name="mix_prompt",
    )(h, h, h, h, h, h, h, att, conv_w)


def _mix_sample_kernel(f_ref, att_ref, st_ref, cw_ref, o_ref, ns_ref, *, t_len):
    cw = cw_ref[...]
    u = [st_ref[0], st_ref[1]] + [f_ref[4, t] * f_ref[5, t] for t in range(t_len)]
    for t in range(t_len):
        o_ref[t] = _mix(f_ref[3, t], f_ref[6, t], f_ref[7, t], att_ref[t], u[t], u[t + 1], u[t + 2], cw).astype(o_ref.dtype)
    ns_ref[0] = u[t_len]
    ns_ref[1] = u[t_len + 1]


def _mix_sample(fields, att, state, conv_w, layer):
    _, t_len, db, d = fields.shape
    return pl.pallas_call(
        functools.partial(_mix_sample_kernel, t_len=t_len),
        grid=(1,),
        in_specs=[pl.BlockSpec(fields.shape, lambda i: (0, 0, 0, 0)),
                  pl.BlockSpec(att.shape, lambda i: (0, 0, 0)),
                  pl.BlockSpec(state.shape, lambda i: (0, 0, 0)),
                  pl.BlockSpec((None, CONV_W, d), lambda i: (layer, 0, 0))],
        out_specs=[pl.BlockSpec((t_len, db, d), lambda i: (0, 0, 0)),
                   pl.BlockSpec((CONV_W - 1, db, d), lambda i: (0, 0, 0))],
        out_shape=[jax.ShapeDtypeStruct((t_len, db, d), BF16), jax.ShapeDtypeStruct((CONV_W - 1, db, d), F32)],
        compiler_params=_cparams(1),
        name="mix_sample",
    )(fields, att, state, conv_w)


def _outproj_kernel(*refs, n_experts):
    if n_experts:
        mix_ref, x_ref, w_ref, g_ref, r_ref, x1_ref, xn_ref, e_ref, gt_ref = refs
    else:
        mix_ref, x_ref, w_ref, g_ref, x1_ref, xn_ref = refs
    x1 = x_ref[...] + jnp.dot(mix_ref[...], w_ref[...], preferred_element_type=F32)
    x1_ref[...] = x1
    xn = _rmsnorm(x1, g_ref[...])
    xn_ref[...] = xn.astype(BF16)
    if n_experts:
        logits = jnp.dot(xn, r_ref[...], precision=lax.Precision.HIGHEST, preferred_element_type=F32)
        col = lax.broadcasted_iota(I32, logits.shape, 1)
        logits = jnp.where(col < n_experts, logits, -jnp.inf)
        m1 = jnp.max(logits, axis=-1, keepdims=True)
        i1 = jnp.min(jnp.where(logits == m1, col, LANE), axis=-1, keepdims=True)
        rest = jnp.where(col == i1, -jnp.inf, logits)
        m2 = jnp.max(rest, axis=-1, keepdims=True)
        i2 = jnp.min(jnp.where(rest == m2, col, LANE), axis=-1, keepdims=True)
        e2 = jnp.exp(m2 - m1)
        den = 1.0 + e2
        e_ref[...] = jnp.where(col == 0, i1, i2)
        gt_ref[...] = jnp.where(col == 0, 1.0 / den, e2 / den)


def _outproj(mix, x, w, g, layer, router=None, n_experts=0):
    rows, d = x.shape
    tm = _row_tile(rows, 256)
    rowspec = pl.BlockSpec((tm, d), lambda i: (i, 0))
    in_specs = [rowspec, rowspec,
                pl.BlockSpec((None, d, d), lambda i: (layer, 0, 0), pipeline_mode=pl.Buffered(1)),
                pl.BlockSpec((None, 1, d), lambda i: (layer, 0, 0))]
    out_specs = [rowspec, rowspec]
    out_shape = [jax.ShapeDtypeStruct((rows, d), F32), jax.ShapeDtypeStruct((rows, d), BF16)]
    args = [mix, x, w, g]
    if n_experts:
        lanespec = pl.BlockSpec((tm, LANE), lambda i: (i, 0))
        in_specs.append(pl.BlockSpec((d, LANE), lambda i: (0, 0)))
        out_specs += [lanespec, lanespec]
        out_shape += [jax.ShapeDtypeStruct((rows, LANE), I32), jax.ShapeDtypeStruct((rows, LANE), F32)]
        args.append(router)
    return pl.pallas_call(
        functools.partial(_outproj_kernel, n_experts=n_experts),
        grid=(rows // tm,),
        in_specs=in_specs, out_specs=out_specs, out_shape=out_shape,
        compiler_params=_cparams(1),
        name="outproj",
    )(*args)


def _swiglu_kernel(ce_ref, cb_ref, cr_ref, x_ref, w1_ref, w3_ref, w2_ref, s_ref, o_ref, acc_ref, *, n_f, bm, residual):
    del ce_ref, cb_ref
    c, f = pl.program_id(0), pl.program_id(1)
    rows = cr_ref[c]
    nsub = (rows + (ROW_TILE - 1)) // ROW_TILE

    def sub(sb, carry):
        r0 = pl.multiple_of(sb * ROW_TILE, ROW_TILE)
        xs = x_ref[pl.ds(r0, ROW_TILE), :]
        h1 = jnp.dot(xs, w1_ref[...], preferred_element_type=F32)
        h3 = jnp.dot(xs, w3_ref[...], preferred_element_type=F32)
        hh = (jax.nn.silu(h1) * h3).astype(BF16)
        part = jnp.dot(hh, w2_ref[...], preferred_element_type=F32)

        @pl.when(f == 0)
        def _():
            acc_ref[pl.ds(r0, ROW_TILE), :] = part

        @pl.when(f > 0)
        def _():
            acc_ref[pl.ds(r0, ROW_TILE), :] += part

        return carry

    lax.fori_loop(0, nsub, sub, 0)

    @pl.when(f == n_f - 1)
    def _():
        def fin(sb, carry):
            r0 = pl.multiple_of(sb * ROW_TILE, ROW_TILE)
            a = acc_ref[pl.ds(r0, ROW_TILE), :]
            if residual:
                o_ref[pl.ds(r0, ROW_TILE), :] = s_ref[pl.ds(r0, ROW_TILE), :] + a
            else:
                o_ref[pl.ds(r0, ROW_TILE), :] = a * s_ref[pl.ds(r0, ROW_TILE), :]
            return carry

        lax.fori_loop(0, nsub, fin, 0)

        def blank(sb, carry):
            r0 = pl.multiple_of(sb * ROW_TILE, ROW_TILE)
            o_ref[pl.ds(r0, ROW_TILE), :] = jnp.zeros((ROW_TILE, o_ref.shape[1]), F32)
            return carry

        lax.fori_loop(nsub, bm // ROW_TILE, blank, 0)


def _swiglu(x, w1, w3, w2, side, chunk_e, chunk_blk, chunk_rows, wsel, bm, residual):
    d = x.shape[1]
    ff = w1.shape[-1]
    tf = _col_tile(ff, 512)
    n_f = ff // tf
    n_chunks = chunk_e.shape[0]
    lead = len(wsel)
    per_expert = w1.ndim == lead + 3

    def fidx(c, f, cr):
        return jnp.where(cr[c] > 0, f, n_f - 1)

    def w13_map(c, f, ce, cb, cr):
        return wsel + ((ce[c],) if per_expert else ()) + (0, fidx(c, f, cr))

    def w2_map(c, f, ce, cb, cr):
        return wsel + ((ce[c],) if per_expert else ()) + (fidx(c, f, cr), 0)

    nlead = lead + (1 if per_expert else 0)
    w13_spec = pl.BlockSpec((None,) * nlead + (d, tf), w13_map)
    w2_spec = pl.BlockSpec((None,) * nlead + (tf, d), w2_map)
    rowmap = lambda c, f, ce, cb, cr: (cb[c], 0)
    side_spec = pl.BlockSpec((bm, side.shape[1]), rowmap)
    return pl.pallas_call(
        functools.partial(_swiglu_kernel, n_f=n_f, bm=bm, residual=residual),
        grid_spec=pltpu.PrefetchScalarGridSpec(
            num_scalar_prefetch=3,
            grid=(n_chunks, n_f),
            in_specs=[pl.BlockSpec((bm, d), rowmap), w13_spec, w13_spec, w2_spec, side_spec],
            out_specs=pl.BlockSpec((bm, d), lambda c, f, ce, cb, cr: (c, 0)),
            scratch_shapes=[pltpu.VMEM((bm, d), F32)],
        ),
        out_shape=jax.ShapeDtypeStruct((x.shape[0], d), F32),
        compiler_params=_cparams(2),
        name="swiglu",
    )(chunk_e, chunk_blk, chunk_rows, x, w1, w3, w2, side)


def _gather_rows_kernel(tok_ref, act_ref, x_hbm, o_hbm, zero_ref, sem, zsem):
    sb = pl.program_id(0)
    base = sb * ROW_TILE

    @pl.when(act_ref[sb] > 0)
    def _():
        def start(r, carry):
            pltpu.make_async_copy(x_hbm.at[tok_ref[base + r]], o_hbm.at[base + r], sem).start()
            return carry

        lax.fori_loop(0, ROW_TILE, start, 0)

        def wait(r, carry):
            pltpu.make_async_copy(x_hbm.at[0], o_hbm.at[base + r], sem).wait()
            return carry

        lax.fori_loop(0, ROW_TILE, wait, 0)

    @pl.when(act_ref[sb] == 0)
    def _():
        zero_ref[...] = jnp.zeros_like(zero_ref)
        cp = pltpu.make_async_copy(zero_ref, o_hbm.at[pl.ds(base, ROW_TILE)], zsem)
        cp.start()
        cp.wait()


def _gather_rows(x, tok_sorted, sub_active):
    n_out = tok_sorted.shape[0]
    d = x.shape[1:]
    return pl.pallas_call(
        _gather_rows_kernel,
        grid_spec=pltpu.PrefetchScalarGridSpec(
            num_scalar_prefetch=2,
            grid=(n_out // ROW_TILE,),
            in_specs=[pl.BlockSpec(memory_space=pl.ANY)],
            out_specs=pl.BlockSpec(memory_space=pl.ANY),
            scratch_shapes=[pltpu.VMEM((ROW_TILE,) + d, x.dtype), pltpu.SemaphoreType.DMA(()), pltpu.SemaphoreType.DMA(())],
        ),
        out_shape=jax.ShapeDtypeStruct((n_out,) + d, x.dtype),
        compiler_params=_cparams(1),
        name="gather_rows",
    )(tok_sorted, sub_active, x)


def _combine_kernel(dest_ref, x_ref, y_hbm, o_ref, buf_ref, sem):
    i = pl.program_id(0)
    tm = x_ref.shape[0]
    base = i * tm * MOE_TOPK

    def start(r, carry):
        for k in range(MOE_TOPK):
            pltpu.make_async_copy(y_hbm.at[dest_ref[base + r * MOE_TOPK + k]], buf_ref.at[k, r], sem).start()
        return carry

    lax.fori_loop(0, tm, start, 0)

    def wait(r, carry):
        for k in range(MOE_TOPK):
            pltpu.make_async_copy(y_hbm.at[0], buf_ref.at[k, r], sem).wait()
        return carry

    lax.fori_loop(0, tm, wait, 0)
    f = buf_ref[0]
    for k in range(1, MOE_TOPK):
        f = f + buf_ref[k]
    o_ref[...] = x_ref[...] + f


def _combine(x, y, dest):
    rows = x.shape[0]
    d = x.shape[1:]
    tm = ROW_TILE
    rowspec = pl.BlockSpec((tm,) + d, lambda i, dest_ref: (i, 0, 0))
    return pl.pallas_call(
        _combine_kernel,
        grid_spec=pltpu.PrefetchScalarGridSpec(
            num_scalar_prefetch=1,
            grid=(rows // tm,),
            in_specs=[rowspec, pl.BlockSpec(memory_space=pl.ANY)],
            out_specs=rowspec,
            scratch_shapes=[pltpu.VMEM((MOE_TOPK, tm) + d, F32), pltpu.SemaphoreType.DMA(())],
        ),
        out_shape=jax.ShapeDtypeStruct(x.shape, F32),
        compiler_params=_cparams(1),
        name="combine",
    )(dest, x, y)


def _route(e_idx, gates, n_experts, bm):
    rows = e_idx.shape[0]
    ns = rows * MOE_TOPK
    n_chunks = ns // bm + n_experts
    flat_e = e_idx.reshape(-1)
    onehot = (flat_e[:, None] == jnp.arange(n_experts, dtype=I32)[None, :]).astype(I32)
    csum = jnp.cumsum(onehot, axis=0)
    rank = jnp.take_along_axis(csum, flat_e[:, None], axis=1)[:, 0] - 1
    counts = csum[-1]
    padded = (counts + bm - 1) // bm * bm
    end_pad = jnp.cumsum(padded)
    start_pad = end_pad - padded
    dest = (start_pad[flat_e] + rank).astype(I32)
    n_used = end_pad[-1] // bm
    cidx = jnp.minimum(jnp.arange(n_chunks, dtype=I32), n_used - 1)
    chunk_e = jnp.minimum(jnp.searchsorted(end_pad, cidx * bm, side='right'), n_experts - 1).astype(I32)
    rows_left = counts[chunk_e] - (cidx * bm - start_pad[chunk_e])
    chunk_rows = jnp.where(jnp.arange(n_chunks) < n_used, jnp.clip(rows_left, 0, bm), 0).astype(I32)
    n_buf = n_chunks * bm
    tok_sorted = jnp.zeros((n_buf,), I32).at[dest].set(jnp.arange(ns, dtype=I32) // MOE_TOPK)
    gate_sorted = jnp.zeros((n_buf,), F32).at[dest].set(gates.reshape(-1))
    sub_per = bm // ROW_TILE
    sub_in_chunk = jnp.arange(n_chunks * sub_per, dtype=I32) % sub_per
    sub_active = (sub_in_chunk * ROW_TILE < jnp.repeat(chunk_rows, sub_per)).astype(I32)
    return dest, tok_sorted, gate_sorted[:, None], sub_active, chunk_e, cidx.astype(I32), chunk_rows


def _ple_kernel(*refs, final):
    if final:
        x_ref, p_ref, wg_ref, wp_ref, g_ref, gf_ref, o_ref, y_ref = refs
    else:
        x_ref, p_ref, wg_ref, wp_ref, g_ref, o_ref = refs
    x = x_ref[...]
    z = jnp.dot(_rmsnorm(x, g_ref[...]).astype(BF16), wg_ref[...], preferred_element_type=F32)
    pp = jnp.dot(p_ref[...].astype(BF16), wp_ref[...], preferred_element_type=F32)
    x3 = x + jax.nn.sigmoid(z) * pp
    o_ref[...] = x3
    if final:
        y_ref[...] = _rmsnorm(x3, gf_ref[...])


def _ple(x, p, wg, wp, g, layer, g_final=None):
    rows, d = x.shape
    pd = p.shape[-1]
    tm = _row_tile(rows, 256)
    final = g_final is not None
    rowspec = pl.BlockSpec((tm, d), lambda i: (i, 0))
    in_specs = [rowspec,
                pl.BlockSpec((None, tm, pd), lambda i: (layer, i, 0)),
                pl.BlockSpec((None, d, d), lambda i: (layer, 0, 0), pipeline_mode=pl.Buffered(1)),
                pl.BlockSpec((None, pd, d), lambda i: (layer, 0, 0), pipeline_mode=pl.Buffered(1)),
                pl.BlockSpec((None, 1, d), lambda i: (layer, 0, 0))]
    args = [x, p, wg, wp, g]
    out_specs, out_shape = [rowspec], [jax.ShapeDtypeStruct((rows, d), F32)]
    if final:
        in_specs.append(pl.BlockSpec((1, d), lambda i: (0, 0)))
        args.append(g_final)
        out_specs.append(rowspec)
        out_shape.append(jax.ShapeDtypeStruct((rows, d), F32))
    res = pl.pallas_call(
        functools.partial(_ple_kernel, final=final),
        grid=(rows // tm,),
        in_specs=in_specs, out_specs=out_specs, out_shape=out_shape,
        compiler_params=_cparams(1),
        name="ple",
    )(*args)
    return res if final else (res[0], None)


def kernel(x_prompt, x_sample, cache_k, cache_v, state_conv, page_table, p_prompt, p_sample, norm_mix, w_in, conv_w,
           w_out, norm_ffn, dense_w1, dense_w3, dense_w2, moe_router, moe_w1, moe_w3, moe_w2, norm_ple, w_ple_gate,
           w_ple_proj, norm_final):
    b, s, d = x_prompt.shape
    db, t_len, _ = x_sample.shape
    depth = w_in.shape[0]
    n_heads, page, hd = cache_k.shape[2:]
    n_pages = page_table.shape[1]
    n_experts = moe_router.shape[-1]
    ppb = BLOCK // page
    past_len = n_pages * page
    assert n_heads * hd == d and w_in.shape[-1] == N_IN_FIELDS * d and BLOCK % page == 0
    assert past_len % BLOCK == 0, "cached tail pages of the own block are not supported"
    assert s % ROW_TILE == 0 and s % page == 0 and t_len <= SUBLANE

    bs, ds = b * s, db * t_len
    rows = -(-(bs + ds) // ROW_TILE) * ROW_TILE
    tp = SUBLANE
    nbk = n_pages // ppb
    n_sel = TOPK_BLOCKS * ppb

    slopes = jnp.exp2(-8.0 * jnp.arange(1, n_heads + 1, dtype=F32) / n_heads)
    w_in_b, w_out_b = w_in.astype(BF16), w_out.astype(BF16)
    dw1, dw3, dw2 = dense_w1.astype(BF16), dense_w3.astype(BF16), dense_w2.astype(BF16)
    mw1, mw3, mw2 = moe_w1.astype(BF16), moe_w3.astype(BF16), moe_w2.astype(BF16)
    wg_b, wp_b = w_ple_gate.astype(BF16), w_ple_proj.astype(BF16)
    router_pad = jnp.pad(moe_router, ((0, 0), (0, 0), (0, LANE - n_experts)))
    g_mix, g_ffn, g_ple = (g.reshape(depth, 1, d) for g in (norm_mix, norm_ffn, norm_ple))

    x = jnp.concatenate([x_prompt.reshape(bs, d), x_sample.reshape(ds, d), jnp.zeros((rows - bs - ds, d), F32)], axis=0)
    p_dim = p_prompt.shape[-1]
    p_all = jnp.concatenate([p_prompt.reshape(depth, bs, p_dim), p_sample.reshape(depth, ds, p_dim),
                             jnp.zeros((depth, rows - bs - ds, p_dim), F32)], axis=1)

    means_t = _block_means(cache_k, page_table).transpose(0, 1, 3, 2, 4)
    state_t = state_conv.transpose(0, 2, 1, 3)
    batch_ix = jnp.arange(db)[:, None, None, None, None]
    page_off = jnp.arange(ppb)

    bm_moe = 4 * ROW_TILE
    bm_dense = _row_tile(rows, 768)
    n_dense_chunks = rows // bm_dense
    dense_tables = (jnp.zeros((n_dense_chunks,), I32), jnp.arange(n_dense_chunks, dtype=I32),
                    jnp.full((n_dense_chunks,), bm_dense, I32))

    kp, vp, cp, ksm, vsm, csm = [], [], [], [], [], []
    y = None
    for i in range(depth):
        h = _inproj(x, g_mix, w_in_b, i)

        att_p = _moba_prompt(h, slopes, b, s, n_heads, hd)
        mix_p, u_tail = _mix_prompt(h, att_p, conv_w, i, b, s, d)

        hs = h[bs:bs + ds]
        qkv = hs[:, :3 * d].reshape(db, t_len, 3, n_heads, hd).transpose(2, 0, 3, 1, 4)
        qkv_p = jnp.pad(qkv, ((0, 0), (0, 0), (0, 0), (0, tp - t_len), (0, 0)))
        picks = _sample_select(qkv_p[0], means_t, i)[:, :, :t_len, :TOPK_BLOCKS]
        picks = picks.transpose(0, 2, 1, 3)
        cols = picks[..., None] * ppb + page_off
        phys = page_table[batch_ix, cols]
        att_s = _sample_attend(qkv_p[0], qkv_p[1], qkv_p[2], cache_k, cache_v, phys.reshape(-1).astype(I32),
                               picks.reshape(-1).astype(I32), slopes, i, t_len, past_len)
        att_s = att_s[:, :, :t_len].transpose(2, 0, 1, 3).reshape(t_len, db, d)
        fields_s = hs.reshape(db, t_len, N_IN_FIELDS, d).transpose(2, 1, 0, 3)
        mix_s, new_state = _mix_sample(fields_s, att_s, state_t[i], conv_w, i)

        mix = jnp.concatenate([mix_p, mix_s.transpose(1, 0, 2).reshape(ds, d), jnp.zeros((rows - bs - ds, d), BF16)], axis=0)

        if i % 2 == 0:
            x1, xn = _outproj(mix, x, w_out_b, g_ffn, i)
            x2 = _swiglu(xn, dw1, dw3, dw2, x1, *dense_tables, wsel=(i // 2,), bm=bm_dense, residual=True)
        else:
            x1, xn, e_idx, gates = _outproj(mix, x, w_out_b, g_ffn, i, router_pad[i // 2], n_experts)
            dest, tok_sorted, gate_sorted, sub_active, chunk_e, chunk_blk, chunk_rows = _route(
                e_idx[:, :MOE_TOPK], gates[:, :MOE_TOPK], n_experts, bm_moe)
            xs = _gather_rows(xn.reshape(rows, d // LANE, LANE), tok_sorted, sub_active).reshape(-1, d)
            ys = _swiglu(xs, mw1, mw3, mw2, gate_sorted, chunk_e, chunk_blk, chunk_rows, wsel=(i // 2,), bm=bm_moe,
                         residual=False)
            x2 = _combine(x1.reshape(rows, d // LANE, LANE), ys.reshape(-1, d // LANE, LANE), dest).reshape(rows, d)

        x, y = _ple(x2, p_all, wg_b, wp_b, g_ple, i, norm_final.reshape(1, d) if i == depth - 1 else None)

        kv = h[:bs, d:3 * d].reshape(b, s // page, page, 2, n_heads, hd).transpose(3, 0, 1, 4, 2, 5)
        kp.append(kv[0]); vp.append(kv[1])
        cp.append(u_tail[:, SUBLANE - (CONV_W - 1):, :])
        ksm.append(qkv[1]); vsm.append(qkv[2])
        csm.append(new_state.transpose(1, 0, 2))

    y_prompt = y[:bs].reshape(b, s, d)
    y_sample = y[bs:bs + ds].reshape(db, t_len, d)
    return (y_prompt, y_sample, jnp.stack(kp), jnp.stack(vp), jnp.stack(cp), jnp.stack(ksm), jnp.stack(vsm), jnp.stack(csm))
```

```python
import functools
import math

import jax
import jax.numpy as jnp
from jax import lax
from jax.experimental import pallas as pl
from jax.experimental.pallas import tpu as pltpu

F32 = jnp.float32
BF16 = jnp.bfloat16
I32 = jnp.int32

RMS_EPS = 1e-6
NEG = -1e30
BLOCK = 256
TOPK_BLOCKS = 3
MOE_TOPK = 2
CONV_W = 3
N_IN_FIELDS = 8
LOG2E = math.log2(math.e)

LANE = 128
SUBLANE = 8
ROW_TILE = 256
VMEM_LIMIT = 56 * 1024 * 1024
N_ALIBI_TERMS = 3


def _cparams(n_axes, vmem=VMEM_LIMIT):
    return pltpu.CompilerParams(dimension_semantics=("arbitrary",) * n_axes, vmem_limit_bytes=vmem)


def _row_tile(rows, max_tile):
    n = rows // ROW_TILE
    best = 1
    for k in range(1, n + 1):
        if n % k == 0 and k * ROW_TILE <= max_tile:
            best = k
    return best * ROW_TILE


def _col_tile(cols, max_tile):
    best = LANE
    for t in range(LANE, min(cols, max_tile) + 1, LANE):
        if cols % t == 0:
            best = t
    return best


def _rmsnorm(x, g):
    ms = jnp.mean(x * x, axis=-1, keepdims=True)
    return x * lax.rsqrt(ms + RMS_EPS) * g


def _dot_nt(a, b):
    return lax.dot_general(a, b, (((1,), (1,)), ((), ())), preferred_element_type=F32)


def _split_bf16(x, n=2):
    terms = []
    for _ in range(n - 1):
        t = x.astype(BF16)
        terms.append(t)
        x = x - t.astype(F32)
    terms.append(x.astype(BF16))
    return terms


def _block_of(pos):
    assert BLOCK & (BLOCK - 1) == 0
    return lax.shift_right_logical(pos, BLOCK.bit_length() - 1)


def _top_picks(cand, col, n, sentinel, axis=-1):
    picks = []
    colf = col.astype(F32)
    for _ in range(n):
        m = jnp.max(cand, axis=axis, keepdims=True)
        first = jnp.min(jnp.where(cand == m, colf, float(sentinel)), axis=axis, keepdims=True)
        picks.append(first.astype(I32))
        cand = jnp.where(colf == first, -jnp.inf, cand)
    return picks


def _inproj_kernel(x_ref, g_ref, w_ref, o_ref, xn_ref):
    @pl.when(pl.program_id(1) == 0)
    def _():
        xn_ref[...] = _rmsnorm(x_ref[...], g_ref[...]).astype(BF16)

    o_ref[...] = jnp.dot(xn_ref[...], w_ref[...], preferred_element_type=F32)


def _inproj(x, g, w, layer):
    rows, d = x.shape
    n = w.shape[-1]
    tm = _row_tile(rows, 768)
    tn = _col_tile(n, 1024)
    return pl.pallas_call(
        _inproj_kernel,
        grid=(rows // tm, n // tn),
        in_specs=[
            pl.BlockSpec((tm, d), lambda i, j: (i, 0)),
            pl.BlockSpec((None, 1, d), lambda i, j: (layer, 0, 0)),
            pl.BlockSpec((None, d, tn), lambda i, j: (layer, 0, j)),
        ],
        out_specs=pl.BlockSpec((tm, tn), lambda i, j: (i, j)),
        out_shape=jax.ShapeDtypeStruct((rows, n), F32),
        scratch_shapes=[pltpu.VMEM((tm, d), BF16)],
        compiler_params=_cparams(2),
        name="inproj",
    )(x, g, w)


def _moba_prompt_kernel(slopes_ref, q_ref, k_ref, v_ref, o_ref, ko_ref, vo_ref, kx_ref, vx_ref, mt_ref, pen_ref,
                        *, nb, hd, nbp, sel_rows):
    s_len = nb * BLOCK
    slope2 = slopes_ref[pl.program_id(1)] * LOG2E
    qscale = hd ** -0.5 * LOG2E
    k = k_ref[...]
    v = v_ref[...]
    ko_ref[...] = k.reshape(ko_ref.shape)
    vo_ref[...] = v.reshape(vo_ref.shape)

    lane = lax.broadcasted_iota(I32, (s_len, hd), 1)
    row = lax.broadcasted_iota(I32, (s_len, hd), 0)
    a_terms = _split_bf16(slope2 * row.astype(F32), N_ALIBI_TERMS)
    kext = jnp.where(lane == _block_of(row), 1.0, 0.0).astype(BF16)
    for t, a in enumerate(a_terms):
        kext = jnp.where(lane == nb + t, a, kext)
    kx_ref[...] = jnp.concatenate([k.astype(BF16), kext], axis=1)
    vx_ref[...] = jnp.concatenate([v.astype(BF16), jnp.where(lane == 0, 1.0, 0.0).astype(BF16)], axis=1)

    means = jnp.mean(k.reshape(nb, BLOCK, hd), axis=1)
    m_hi, m_lo = _split_bf16(means)
    mt_ref[...] = jnp.zeros_like(mt_ref)
    mt_ref[0:nb, :] = jnp.concatenate([m_hi, m_hi, m_lo], axis=1)

    col = lax.broadcasted_iota(I32, (sel_rows, nbp), 1)
    rsel = lax.broadcasted_iota(I32, (sel_rows, nbp), 0)
    ones_lane = (col >= nb) & (col < nb + N_ALIBI_TERMS)

    def select(c, carry):
        row0 = pl.multiple_of(c * sel_rows, sel_rows)
        q_hi, q_lo = _split_bf16(q_ref[pl.ds(row0, sel_rows), :])
        sc = _dot_nt(jnp.concatenate([q_hi, q_lo, q_hi], axis=1), mt_ref[...])
        n_past = _block_of(row0 + rsel)
        picks = _top_picks(jnp.where(col < n_past, sc, NEG), col, TOPK_BLOCKS, nbp)
        pen = jnp.where(col == n_past, 0.0, NEG)
        for r, p in enumerate(picks):
            pen = jnp.where((col == p) & (r < n_past), 0.0, pen)
        pen_ref[pl.ds(row0, sel_rows), :] = jnp.where(ones_lane, 1.0, pen).astype(BF16)
        return carry

    lax.fori_loop(0, s_len // sel_rows, select, 0)

    r_io = lax.broadcasted_iota(I32, (BLOCK, BLOCK), 0)
    c_io = lax.broadcasted_iota(I32, (BLOCK, BLOCK), 1)

    for i in range(nb):
        row0, win = i * BLOCK, i * BLOCK
        qx = jnp.concatenate([(q_ref[row0:row0 + BLOCK, :] * qscale).astype(BF16), pen_ref[row0:row0 + BLOCK, :]], axis=1)
        s_all = _dot_nt(qx, kx_ref[0:win + BLOCK, :])
        s_own = jnp.where(r_io >= c_io, s_all[:, win:], NEG)
        s_all = jnp.concatenate([s_all[:, :win], s_own], axis=1) if i else s_own
        m = jnp.max(s_all, axis=-1, keepdims=True)
        p = jnp.exp2(s_all - m).astype(BF16)
        o2 = jnp.dot(p, vx_ref[0:win + BLOCK, :], preferred_element_type=F32)
        o_ref[row0:row0 + BLOCK, :] = o2[:, :hd] / o2[:, hd:hd + 1]


def _moba_prompt(h, slopes, b, s, n_heads, hd, page):
    assert s % BLOCK == 0 and TOPK_BLOCKS == 3
    nb = s // BLOCK
    nbp = hd
    assert nb + N_ALIBI_TERMS <= nbp
    sel_rows = min(s, 1024)
    assert s % sel_rows == 0
    kern = functools.partial(_moba_prompt_kernel, nb=nb, hd=hd, nbp=nbp, sel_rows=sel_rows)
    pages = jax.ShapeDtypeStruct((b, s // page, n_heads, page, hd), F32)
    page_spec = pl.BlockSpec((None, s // page, None, page, hd), lambda bi, hi: (bi, 0, hi, 0, 0))
    return pl.pallas_call(
        kern,
        grid=(b, n_heads),
        in_specs=[
            pl.BlockSpec(memory_space=pltpu.SMEM),
            pl.BlockSpec((s, hd), lambda bi, hi: (bi, hi)),
            pl.BlockSpec((s, hd), lambda bi, hi: (bi, n_heads + hi)),
            pl.BlockSpec((s, hd), lambda bi, hi: (bi, 2 * n_heads + hi)),
        ],
        out_specs=[pl.BlockSpec((s, hd), lambda bi, hi: (bi, hi)), page_spec, page_spec],
        out_shape=[jax.ShapeDtypeStruct((b * s, n_heads * hd), F32), pages, pages],
        scratch_shapes=[
            pltpu.VMEM((s, 2 * hd), BF16),
            pltpu.VMEM((s, 2 * hd), BF16),
            pltpu.VMEM((nbp, 3 * hd), BF16),
            pltpu.VMEM((s, nbp), BF16),
        ],
        compiler_params=_cparams(2),
        name="moba_prompt",
    )(slopes, h, h, h)


def _block_means_kernel(pt_ref, *refs, ppb, page, bps):
    del pt_ref
    o_ref = refs[ppb * bps]
    for blk in range(bps):
        tot = jnp.sum(refs[blk * ppb][0, 0], axis=1)
        for r in refs[blk * ppb + 1:(blk + 1) * ppb]:
            tot = tot + jnp.sum(r[0, 0], axis=1)
        o_ref[0, 0, blk] = tot * (1.0 / (ppb * page))


def _block_means(cache_k, page_table):
    depth, _, n_heads, page, hd = cache_k.shape
    db, n_pages = page_table.shape
    ppb = BLOCK // page
    nbk = n_pages // ppb
    bps = 2 if nbk % 2 == 0 else 1
    pt = page_table.reshape(-1)

    def page_map(jp, l, b, n, pt_ref):
        return (l, pt_ref[b * n_pages + n * (ppb * bps) + jp], 0, 0, 0)

    return pl.pallas_call(
        functools.partial(_block_means_kernel, ppb=ppb, page=page, bps=bps),
        grid_spec=pltpu.PrefetchScalarGridSpec(
            num_scalar_prefetch=1,
            grid=(depth, db, nbk // bps),
            in_specs=[pl.BlockSpec((1, 1, n_heads, page, hd), functools.partial(page_map, jp))
                      for jp in range(ppb * bps)],
            out_specs=pl.BlockSpec((1, 1, bps, n_heads, hd), lambda l, b, n, pt_ref: (l, b, n, 0, 0)),
        ),
        out_shape=jax.ShapeDtypeStruct((depth, db, nbk, n_heads, hd), F32),
        compiler_params=_cparams(3),
        name="block_means",
    )(pt, *([cache_k] * (ppb * bps)))


def _sample_select_kernel(q_ref, m_ref, o_ref, *, nbk):
    q = q_ref[0]
    means = m_ref[0, 0]
    sc = lax.dot_general(q, means, (((2,), (2,)), ((0,), (0,))),
                         precision=lax.Precision.HIGHEST, preferred_element_type=F32)
    col = lax.broadcasted_iota(I32, sc.shape, 2)
    picks = _top_picks(sc, col, TOPK_BLOCKS, nbk)
    lane = lax.broadcasted_iota(I32, o_ref.shape[1:], 2)
    o_ref[0] = jnp.where(lane == 0, picks[0], jnp.where(lane == 1, picks[1], picks[2]))


def _sample_select(q_s, means_t, layer):
    db, n_heads, tp, hd = q_s.shape
    nbk = means_t.shape[3]
    assert nbk >= TOPK_BLOCKS
    return pl.pallas_call(
        functools.partial(_sample_select_kernel, nbk=nbk),
        grid=(db,),
        in_specs=[
            pl.BlockSpec((1, n_heads, tp, hd), lambda b: (b, 0, 0, 0)),
            pl.BlockSpec((1, 1, n_heads, nbk, hd), lambda b: (layer, b, 0, 0, 0)),
        ],
        out_specs=pl.BlockSpec((1, n_heads, tp, LANE), lambda b: (b, 0, 0, 0)),
        out_shape=jax.ShapeDtypeStruct((db, n_heads, tp, LANE), I32),
        compiler_params=_cparams(1),
        name="sample_select",
    )(q_s, means_t)


def _sample_attend_kernel(phys_ref, blk_ref, slopes_ref, q_ref, kn_ref, vn_ref, ck_hbm, cv_hbm, o_ref, kbuf, vbuf, sems,
                          *, layer, t_len, n_sel, past_len, hd, page):
    b, hh = pl.program_id(0), pl.program_id(1)
    n_heads = pl.num_programs(1)
    step = b * n_heads + hh
    n_steps = pl.num_programs(0) * n_heads

    def page_copies(st, slot):
        sb, sh = st // n_heads, st % n_heads
        cps = []
        for t in range(t_len):
            for c in range(n_sel):
                pg = phys_ref[((sb * t_len + t) * n_heads + sh) * n_sel + c]
                j = t * n_sel + c
                cps.append(pltpu.make_async_copy(ck_hbm.at[layer, pg, sh], kbuf.at[slot, j], sems.at[0, slot]))
                cps.append(pltpu.make_async_copy(cv_hbm.at[layer, pg, sh], vbuf.at[slot, j], sems.at[1, slot]))
        return cps

    @pl.when(step == 0)
    def _():
        for cp in page_copies(step, 0):
            cp.start()

    @pl.when(step + 1 < n_steps)
    def _():
        for cp in page_copies(step + 1, (step + 1) % 2):
            cp.start()

    slot = step % 2
    for cp in page_copies(step, slot):
        cp.wait()

    slope = slopes_ref[hh]
    scale = hd ** -0.5
    ppb = BLOCK // page
    qb = (q_ref[0, 0] * scale).astype(BF16)
    tp = qb.shape[0]
    s_own_all = _dot_nt(qb, kn_ref[0, 0].astype(BF16))
    vn = vn_ref[0, 0].astype(BF16)
    lane = lax.broadcasted_iota(I32, (1, page), 1)
    tcol = lax.broadcasted_iota(I32, (1, tp), 1)
    outs = []
    for t in range(t_len):
        tq = past_len + t
        pos = jnp.concatenate(
            [blk_ref[((b * t_len + t) * n_heads + hh) * TOPK_BLOCKS + c // ppb] * BLOCK + (c % ppb) * page + lane
             for c in range(n_sel)], axis=1)
        k_sel = kbuf[slot, t * n_sel:(t + 1) * n_sel].reshape(n_sel * page, hd).astype(BF16)
        v_sel = vbuf[slot, t * n_sel:(t + 1) * n_sel].reshape(n_sel * page, hd).astype(BF16)
        s_sel = _dot_nt(qb, k_sel)[t:t + 1, :] - slope * (tq - pos).astype(F32)
        s_own = s_own_all[t:t + 1, :] - slope * (t - tcol).astype(F32)
        s_own = jnp.where(tcol <= t, s_own, NEG)
        m = jnp.maximum(jnp.max(s_own, axis=-1, keepdims=True), jnp.max(s_sel, axis=-1, keepdims=True))
        p_own = jnp.exp(s_own - m)
        p_sel = jnp.exp(s_sel - m)
        l = jnp.sum(p_own, axis=-1, keepdims=True) + jnp.sum(p_sel, axis=-1, keepdims=True)
        acc = (jnp.dot(jnp.broadcast_to(p_own, (tp, tp)).astype(BF16), vn, preferred_element_type=F32)
               + jnp.dot(jnp.broadcast_to(p_sel, (tp, n_sel * page)).astype(BF16), v_sel, preferred_element_type=F32))
        outs.append(acc[0:1, :] / l)
    outs += [jnp.zeros((1, hd), F32)] * (tp - t_len)
    o_ref[0, 0] = jnp.concatenate(outs, axis=0)


def _sample_attend(q_s, k_new, v_new, cache_k, cache_v, phys, blk, slopes, layer, t_len, past_len):
    db, n_heads, tp, hd = q_s.shape
    page = cache_k.shape[3]
    n_sel = TOPK_BLOCKS * (BLOCK // page)
    n_pg = t_len * n_sel
    small = pl.BlockSpec((1, 1, tp, hd), lambda b, hh, phys_ref, blk_ref: (b, hh, 0, 0))
    anyspec = pl.BlockSpec(memory_space=pl.ANY)
    kern = functools.partial(_sample_attend_kernel, layer=layer, t_len=t_len, n_sel=n_sel, past_len=past_len, hd=hd,
                             page=page)
    return pl.pallas_call(
        kern,
        grid_spec=pltpu.PrefetchScalarGridSpec(
            num_scalar_prefetch=2,
            grid=(db, n_heads),
            in_specs=[pl.BlockSpec(memory_space=pltpu.SMEM), small, small, small, anyspec, anyspec],
            out_specs=small,
            scratch_shapes=[pltpu.VMEM((2, n_pg, page, hd), F32), pltpu.VMEM((2, n_pg, page, hd), F32),
                            pltpu.SemaphoreType.DMA((2, 2))],
        ),
        out_shape=jax.ShapeDtypeStruct((db, n_heads, tp, hd), F32),
        compiler_params=_cparams(2),
        name="sample_attend",
    )(phys, blk, slopes, q_s, k_new, v_new, cache_k, cache_v)


def _mix(gb, ga, gcv, att, u0, u1, u2, cw):
    conv = cw[0:1] * u0 + cw[1:2] * u1 + cw[2:3] * u2
    return jax.nn.sigmoid(ga) * att + jax.nn.sigmoid(gcv) * (gb * conv)


def _mix_prompt_kernel(gb_ref, gc_ref, xc_ref, ga_ref, gcv_ref, gch_ref, xch_ref, att_ref, cw_ref, o_ref, us_ref,
                       *, tiles_per_seq, n_tiles):
    i = pl.program_id(0)

    @pl.when(i >= n_tiles)
    def _():
        o_ref[...] = jnp.zeros_like(o_ref)

    @pl.when(i < n_tiles)
    def _():
        cw = cw_ref[...]
        u = gc_ref[...] * xc_ref[...]
        halo = jnp.where(i % tiles_per_seq == 0, 0.0, gch_ref[...] * xch_ref[...])
        u1 = pltpu.roll(u, 1, 0)
        u2 = pltpu.roll(u, 2, 0)
        o_ref[...] = _mix(gb_ref[...], ga_ref[...], gcv_ref[...], att_ref[...], u2, u1, u, cw).astype(o_ref.dtype)
        row = lax.broadcasted_iota(I32, (SUBLANE, u.shape[1]), 0)
        f1 = jnp.where(row < 1, pltpu.roll(halo, 1, 0), u1[0:SUBLANE])
        f2 = jnp.where(row < 2, pltpu.roll(halo, 2, 0), u2[0:SUBLANE])
        o_ref[0:SUBLANE, :] = _mix(gb_ref[0:SUBLANE, :], ga_ref[0:SUBLANE, :], gcv_ref[0:SUBLANE, :],
                                   att_ref[0:SUBLANE, :], f2, f1, u[0:SUBLANE], cw).astype(o_ref.dtype)
        us_ref[...] = u[u.shape[0] - SUBLANE:, :]


def _mix_prompt(h, att, conv_w, layer, b, s, d):
    assert CONV_W == 3
    tm = ROW_TILE
    tps = s // tm
    hpt = tm // SUBLANE
    n_tiles = b * tps
    rows = h.shape[0]
    clamp = lambda i: jnp.minimum(i, n_tiles - 1)

    def field(f):
        return pl.BlockSpec((tm, d), lambda i: (clamp(i), f))

    def halo(f):
        return pl.BlockSpec((SUBLANE, d), lambda i: (jnp.maximum(clamp(i) * hpt - 1, 0), f))

    return pl.pallas_call(
        functools.partial(_mix_prompt_kernel, tiles_per_seq=tps, n_tiles=n_tiles),
        grid=(rows // tm,),
        in_specs=[field(3), field(4), field(5), field(6), field(7), halo(4), halo(5),
                  pl.BlockSpec((tm, d), lambda i: (clamp(i), 0)),
                  pl.BlockSpec((None, CONV_W, d), lambda i: (layer, 0, 0))],
        out_specs=[pl.BlockSpec((tm, d), lambda i: (i, 0)),
                   pl.BlockSpec((None, SUBLANE, d), lambda i: (clamp(i) // tps, 0, 0))],
        out_shape=[jax.ShapeDtypeStruct((rows, d), BF16), jax.ShapeDtypeStruct((b, SUBLANE, d), F32)],
        compiler_params=_cparams(1),
        name="mix_prompt",
    )(h, h, h, h, h, h, h, att, conv_w)


def _mix_sample_kernel(f_ref, att_ref, st_ref, cw_ref, o_ref, ns_ref, *, t_len):
    cw = cw_ref[...]
    u = [st_ref[0], st_ref[1]] + [f_ref[4, t] * f_ref[5, t] for t in range(t_len)]
    for t in range(t_len):
        o_ref[t] = _mix(f_ref[3, t], f_ref[6, t], f_ref[7, t], att_ref[t], u[t], u[t + 1], u[t + 2], cw).astype(o_ref.dtype)
    ns_ref[0] = u[t_len]
    ns_ref[1] = u[t_len + 1]


def _mix_sample(fields, att, state, conv_w, layer):
    _, t_len, db, d = fields.shape
    return pl.pallas_call(
        functools.partial(_mix_sample_kernel, t_len=t_len),
        grid=(1,),
        in_specs=[pl.BlockSpec(fields.shape, lambda i: (0, 0, 0, 0)),
                  pl.BlockSpec(att.shape, lambda i: (0, 0, 0)),
                  pl.BlockSpec(state.shape, lambda i: (0, 0, 0)),
                  pl.BlockSpec((None, CONV_W, d), lambda i: (layer, 0, 0))],
        out_specs=[pl.BlockSpec((t_len, db, d), lambda i: (0, 0, 0)),
                   pl.BlockSpec((CONV_W - 1, db, d), lambda i: (0, 0, 0))],
        out_shape=[jax.ShapeDtypeStruct((t_len, db, d), BF16), jax.ShapeDtypeStruct((CONV_W - 1, db, d), F32)],
        compiler_params=_cparams(1),
        name="mix_sample",
    )(fields, att, state, conv_w)


def _outproj_kernel(*refs, n_experts):
    if n_experts:
        mix_ref, x_ref, w_ref, g_ref, r_ref, x1_ref, xn3_ref, e_ref, gt_ref = refs
    else:
        mix_ref, x_ref, w_ref, g_ref, x1_ref, xn_ref = refs
    x1 = x_ref[...] + jnp.dot(mix_ref[...], w_ref[...], preferred_element_type=F32)
    x1_ref[...] = x1
    xn = _rmsnorm(x1, g_ref[...])
    if not n_experts:
        xn_ref[...] = xn.astype(BF16)
        return
    for sl in range(xn3_ref.shape[1]):
        xn3_ref[:, sl, :] = xn[:, sl * LANE:(sl + 1) * LANE]
    x_hi, x_lo = _split_bf16(xn)
    logits = _dot_nt(r_ref[...], jnp.concatenate([x_hi, x_lo, x_hi], axis=1))
    ex = lax.broadcasted_iota(I32, logits.shape, 0)
    logits = jnp.where(ex < n_experts, logits, -jnp.inf)
    (i1, i2) = _top_picks(logits, ex, MOE_TOPK, logits.shape[0], axis=0)
    m1 = jnp.max(logits, axis=0, keepdims=True)
    m2 = jnp.max(jnp.where(ex == i1, -jnp.inf, logits), axis=0, keepdims=True)
    e2 = jnp.exp(m2 - m1)
    den = 1.0 + e2
    orow = lax.broadcasted_iota(I32, e_ref.shape, 0)
    e_ref[...] = jnp.where(orow == 0, i1, i2)
    gt_ref[...] = jnp.where(orow == 0, 1.0 / den, e2 / den)


def _outproj(mix, x, w, g, layer, router_t=None, n_experts=0):
    rows, d = x.shape
    tm = _row_tile(rows, 256)
    rowspec = pl.BlockSpec((tm, d), lambda i: (i, 0))
    in_specs = [rowspec, rowspec,
                pl.BlockSpec((None, d, d), lambda i: (layer, 0, 0), pipeline_mode=pl.Buffered(1)),
                pl.BlockSpec((None, 1, d), lambda i: (layer, 0, 0))]
    args = [mix, x, w, g]
    if n_experts:
        tspec = pl.BlockSpec((SUBLANE, tm), lambda i: (0, i))
        in_specs.append(pl.BlockSpec(router_t.shape, lambda i: (0, 0)))
        out_specs = [rowspec, pl.BlockSpec((tm, d // LANE, LANE), lambda i: (i, 0, 0)), tspec, tspec]
        out_shape = [jax.ShapeDtypeStruct((rows, d), F32), jax.ShapeDtypeStruct((rows, d // LANE, LANE), F32),
                     jax.ShapeDtypeStruct((SUBLANE, rows), I32), jax.ShapeDtypeStruct((SUBLANE, rows), F32)]
        args.append(router_t)
    else:
        out_specs = [rowspec, rowspec]
        out_shape = [jax.ShapeDtypeStruct((rows, d), F32), jax.ShapeDtypeStruct((rows, d), BF16)]
    return pl.pallas_call(
        functools.partial(_outproj_kernel, n_experts=n_experts),
        grid=(rows // tm,),
        in_specs=in_specs, out_specs=out_specs, out_shape=out_shape,
        compiler_params=_cparams(1),
        name="outproj",
    )(*args)


def _swiglu_kernel(ce_ref, cb_ref, cr_ref, x_ref, w1_ref, w3_ref, w2_ref, s_ref, o_ref, acc_ref, *xb_scratch,
                   n_f, bm, routed):
    del ce_ref, cb_ref
    c, f = pl.program_id(0), pl.program_id(1)
    rows = cr_ref[c]
    nsub = (rows + (ROW_TILE - 1)) // ROW_TILE
    n_sl = acc_ref.shape[1] // LANE

    if routed:
        xb_ref, = xb_scratch

        @pl.when(f == 0)
        def _():
            def conv(sb, carry):
                r0 = pl.multiple_of(sb * ROW_TILE, ROW_TILE)
                for sl in range(n_sl):
                    xb_ref[pl.ds(r0, ROW_TILE), sl * LANE:(sl + 1) * LANE] = x_ref[pl.ds(r0, ROW_TILE), sl, :].astype(BF16)
                return carry

            lax.fori_loop(0, nsub, conv, 0)
    else:
        xb_ref = x_ref

    def accumulate(r0, n_rows):
        xs = xb_ref[pl.ds(r0, n_rows), :]
        h1 = jnp.dot(xs, w1_ref[...], preferred_element_type=F32)
        h3 = jnp.dot(xs, w3_ref[...], preferred_element_type=F32)
        hh = (jax.nn.silu(h1) * h3).astype(BF16)
        part = jnp.dot(hh, w2_ref[...], preferred_element_type=F32)

        @pl.when(f == 0)
        def _():
            acc_ref[pl.ds(r0, n_rows), :] = part

        @pl.when(f > 0)
        def _():
            acc_ref[pl.ds(r0, n_rows), :] += part

    @pl.when(nsub == bm // ROW_TILE)
    def _():
        accumulate(0, bm)

    @pl.when(nsub < bm // ROW_TILE)
    def _():
        def sub(sb, carry):
            accumulate(pl.multiple_of(sb * ROW_TILE, ROW_TILE), ROW_TILE)
            return carry

        lax.fori_loop(0, nsub, sub, 0)

    @pl.when(f == n_f - 1)
    def _():
        def fin(sb, carry):
            r0 = pl.multiple_of(sb * ROW_TILE, ROW_TILE)
            a = acc_ref[pl.ds(r0, ROW_TILE), :]
            if routed:
                a = a * s_ref[pl.ds(r0, ROW_TILE), :]
                for sl in range(n_sl):
                    o_ref[pl.ds(r0, ROW_TILE), sl, :] = a[:, sl * LANE:(sl + 1) * LANE]
            else:
                o_ref[pl.ds(r0, ROW_TILE), :] = s_ref[pl.ds(r0, ROW_TILE), :] + a
            return carry

        lax.fori_loop(0, nsub, fin, 0)

        def blank(sb, carry):
            r0 = pl.multiple_of(sb * ROW_TILE, ROW_TILE)
            o_ref[pl.ds(r0, ROW_TILE)] = jnp.zeros((ROW_TILE,) + o_ref.shape[1:], F32)
            return carry

        lax.fori_loop(nsub, bm // ROW_TILE, blank, 0)


def _swiglu(x, w1, w3, w2, side, chunk_e, chunk_blk, chunk_rows, wsel, bm, routed):
    d = w1.shape[-2]
    ff = w1.shape[-1]
    tf = _col_tile(ff, 512)
    n_f = ff // tf
    n_chunks = chunk_e.shape[0]
    lead = len(wsel)
    per_expert = w1.ndim == lead + 3

    def fidx(c, f, cr):
        return jnp.where(cr[c] > 0, f, n_f - 1)

    def w13_map(c, f, ce, cb, cr):
        return wsel + ((ce[c],) if per_expert else ()) + (0, fidx(c, f, cr))

    def w2_map(c, f, ce, cb, cr):
        return wsel + ((ce[c],) if per_expert else ()) + (fidx(c, f, cr), 0)

    nlead = lead + (1 if per_expert else 0)
    w13_spec = pl.BlockSpec((None,) * nlead + (d, tf), w13_map)
    w2_spec = pl.BlockSpec((None,) * nlead + (tf, d), w2_map)
    zeros = (0,) * (x.ndim - 1)
    x_spec = pl.BlockSpec((bm,) + x.shape[1:], lambda c, f, ce, cb, cr: (cb[c],) + zeros)
    o_spec = pl.BlockSpec((bm,) + x.shape[1:], lambda c, f, ce, cb, cr: (c,) + zeros)
    side_spec = pl.BlockSpec((bm, side.shape[1]), lambda c, f, ce, cb, cr: (cb[c], 0))
    scratch = [pltpu.VMEM((bm, d), F32)] + ([pltpu.VMEM((bm, d), BF16)] if routed else [])
    return pl.pallas_call(
        functools.partial(_swiglu_kernel, n_f=n_f, bm=bm, routed=routed),
        grid_spec=pltpu.PrefetchScalarGridSpec(
            num_scalar_prefetch=3,
            grid=(n_chunks, n_f),
            in_specs=[x_spec, w13_spec, w13_spec, w2_spec, side_spec],
            out_specs=o_spec,
            scratch_shapes=scratch,
        ),
        out_shape=jax.ShapeDtypeStruct(x.shape, F32),
        compiler_params=_cparams(2),
        name="swiglu",
    )(chunk_e, chunk_blk, chunk_rows, x, w1, w3, w2, side)


def _gather_rows_kernel(tok_ref, act_ref, x_hbm, o_ref, sem):
    sb = pl.program_id(0)
    base = sb * ROW_TILE

    @pl.when(act_ref[sb] > 0)
    def _():
        def start(r, carry):
            pltpu.make_async_copy(x_hbm.at[tok_ref[base + r]], o_ref.at[r], sem).start()
            return carry

        lax.fori_loop(0, ROW_TILE, start, 0)

        def wait(r, carry):
            pltpu.make_async_copy(x_hbm.at[0], o_ref.at[r], sem).wait()
            return carry

        lax.fori_loop(0, ROW_TILE, wait, 0)

    @pl.when(act_ref[sb] == 0)
    def _():
        o_ref[...] = jnp.zeros_like(o_ref)


def _gather_rows(x, tok_sorted, sub_active):
    n_out = tok_sorted.shape[0]
    d = x.shape[1:]
    return pl.pallas_call(
        _gather_rows_kernel,
        grid_spec=pltpu.PrefetchScalarGridSpec(
            num_scalar_prefetch=2,
            grid=(n_out // ROW_TILE,),
            in_specs=[pl.BlockSpec(memory_space=pl.ANY)],
            out_specs=pl.BlockSpec((ROW_TILE,) + d, lambda sb, tok_ref, act_ref: (sb, 0, 0)),
            scratch_shapes=[pltpu.SemaphoreType.DMA(())],
        ),
        out_shape=jax.ShapeDtypeStruct((n_out,) + d, x.dtype),
        compiler_params=_cparams(1),
        name="gather_rows",
    )(tok_sorted, sub_active, x)


def _combine_kernel(dest_ref, x_ref, y_hbm, o_ref, buf_ref, sem):
    i = pl.program_id(0)
    tm = x_ref.shape[0]
    base = i * tm * MOE_TOPK

    def start(r, carry):
        for k in range(MOE_TOPK):
            pltpu.make_async_copy(y_hbm.at[dest_ref[base + r * MOE_TOPK + k]], buf_ref.at[k, r], sem).start()
        return carry

    lax.fori_loop(0, tm, start, 0)

    def wait(r, carry):
        for k in range(MOE_TOPK):
            pltpu.make_async_copy(y_hbm.at[0], buf_ref.at[k, r], sem).wait()
        return carry

    lax.fori_loop(0, tm, wait, 0)
    for sl in range(buf_ref.shape[2]):
        f = buf_ref[0, :, sl, :]
        for k in range(1, MOE_TOPK):
            f = f + buf_ref[k, :, sl, :]
        o_ref[:, sl * LANE:(sl + 1) * LANE] = x_ref[:, sl * LANE:(sl + 1) * LANE] + f


def _combine(x, y, dest):
    rows, d = x.shape
    tm = ROW_TILE
    rowspec = pl.BlockSpec((tm, d), lambda i, dest_ref: (i, 0))
    return pl.pallas_call(
        _combine_kernel,
        grid_spec=pltpu.PrefetchScalarGridSpec(
            num_scalar_prefetch=1,
            grid=(rows // tm,),
            in_specs=[rowspec, pl.BlockSpec(memory_space=pl.ANY)],
            out_specs=rowspec,
            scratch_shapes=[pltpu.VMEM((MOE_TOPK, tm) + y.shape[1:], F32), pltpu.SemaphoreType.DMA(())],
        ),
        out_shape=jax.ShapeDtypeStruct(x.shape, F32),
        compiler_params=_cparams(1),
        name="combine",
    )(dest, x, y)


def _route(e_idx, gates, n_experts, bm):
    rows = e_idx.shape[0]
    ns = rows * MOE_TOPK
    n_chunks = ns // bm + n_experts
    flat_e = e_idx.reshape(-1)
    onehot = (flat_e[:, None] == jnp.arange(n_experts, dtype=I32)[None, :]).astype(I32)
    csum = jnp.cumsum(onehot, axis=0)
    rank = jnp.take_along_axis(csum, flat_e[:, None], axis=1)[:, 0] - 1
    counts = csum[-1]
    padded = (counts + bm - 1) // bm * bm
    end_pad = jnp.cumsum(padded)
    start_pad = end_pad - padded
    dest = (start_pad[flat_e] + rank).astype(I32)
    n_used = end_pad[-1] // bm
    cidx = jnp.minimum(jnp.arange(n_chunks, dtype=I32), n_used - 1)
    chunk_e = jnp.minimum(jnp.sum((end_pad[None, :] <= (cidx * bm)[:, None]).astype(I32), axis=1), n_experts - 1)
    rows_left = counts[chunk_e] - (cidx * bm - start_pad[chunk_e])
    chunk_rows = jnp.where(jnp.arange(n_chunks) < n_used, jnp.clip(rows_left, 0, bm), 0).astype(I32)
    n_buf = n_chunks * bm
    tok_sorted = jnp.zeros((n_buf,), I32).at[dest].set(jnp.arange(ns, dtype=I32) // MOE_TOPK)
    gate_sorted = jnp.zeros((n_buf,), F32).at[dest].set(gates.reshape(-1))
    sub_per = bm // ROW_TILE
    sub_in_chunk = jnp.arange(n_chunks * sub_per, dtype=I32) % sub_per
    sub_active = (sub_in_chunk * ROW_TILE < jnp.repeat(chunk_rows, sub_per)).astype(I32)
    return dest, tok_sorted, gate_sorted[:, None], sub_active, chunk_e.astype(I32), cidx.astype(I32), chunk_rows


def _ple_kernel(*refs, final):
    if final:
        x_ref, p_ref, wg_ref, wp_ref, g_ref, gf_ref, o_ref, y_ref = refs
    else:
        x_ref, p_ref, wg_ref, wp_ref, g_ref, o_ref = refs
    x = x_ref[...]
    z = jnp.dot(_rmsnorm(x, g_ref[...]).astype(BF16), wg_ref[...], preferred_element_type=F32)
    pp = jnp.dot(p_ref[...].astype(BF16), wp_ref[...], preferred_element_type=F32)
    x3 = x + jax.nn.sigmoid(z) * pp
    o_ref[...] = x3
    if final:
        y_ref[...] = _rmsnorm(x3, gf_ref[...])


def _ple(x, p, wg, wp, g, layer, g_final=None):
    rows, d = x.shape
    pd = p.shape[-1]
    tm = _row_tile(rows, 256)
    final = g_final is not None
    rowspec = pl.BlockSpec((tm, d), lambda i: (i, 0))
    in_specs = [rowspec,
                pl.BlockSpec((None, tm, pd), lambda i: (layer, i, 0)),
                pl.BlockSpec((None, d, d), lambda i: (layer, 0, 0), pipeline_mode=pl.Buffered(1)),
                pl.BlockSpec((None, pd, d), lambda i: (layer, 0, 0), pipeline_mode=pl.Buffered(1)),
                pl.BlockSpec((None, 1, d), lambda i: (layer, 0, 0))]
    args = [x, p, wg, wp, g]
    out_specs, out_shape = [rowspec], [jax.ShapeDtypeStruct((rows, d), F32)]
    if final:
        in_specs.append(pl.BlockSpec((1, d), lambda i: (0, 0)))
        args.append(g_final)
        out_specs.append(rowspec)
        out_shape.append(jax.ShapeDtypeStruct((rows, d), F32))
    res = pl.pallas_call(
        functools.partial(_ple_kernel, final=final),
        grid=(rows // tm,),
        in_specs=in_specs, out_specs=out_specs, out_shape=out_shape,
        compiler_params=_cparams(1),
        name="ple",
    )(*args)
    return res if final else (res[0], None)


def kernel(x_prompt, x_sample, cache_k, cache_v, state_conv, page_table, p_prompt, p_sample, norm_mix, w_in, conv_w,
           w_out, norm_ffn, dense_w1, dense_w3, dense_w2, moe_router, moe_w1, moe_w3, moe_w2, norm_ple, w_ple_gate,
           w_ple_proj, norm_final):
    b, s, d = x_prompt.shape
    db, t_len, _ = x_sample.shape
    depth = w_in.shape[0]
    n_heads, page, hd = cache_k.shape[2:]
    n_pages = page_table.shape[1]
    n_experts = moe_router.shape[-1]
    ppb = BLOCK // page
    past_len = n_pages * page
    assert n_heads * hd == d and w_in.shape[-1] == N_IN_FIELDS * d and BLOCK % page == 0 and hd == LANE
    assert past_len % BLOCK == 0, "cached tail pages of the own block are not supported"
    assert s % ROW_TILE == 0 and s % page == 0 and t_len <= SUBLANE

    bs, ds = b * s, db * t_len
    rows = -(-(bs + ds) // ROW_TILE) * ROW_TILE
    tp = SUBLANE

    slopes = jnp.exp2(-8.0 * jnp.arange(1, n_heads + 1, dtype=F32) / n_heads)
    w_in_b, w_out_b = w_in.astype(BF16), w_out.astype(BF16)
    dw1, dw3, dw2 = dense_w1.astype(BF16), dense_w3.astype(BF16), dense_w2.astype(BF16)
    mw1, mw3, mw2 = moe_w1.astype(BF16), moe_w3.astype(BF16), moe_w2.astype(BF16)
    wg_b, wp_b = w_ple_gate.astype(BF16), w_ple_proj.astype(BF16)
    n_exp_pad = 2 * SUBLANE
    assert n_experts <= n_exp_pad
    router_t = jnp.pad(moe_router.transpose(0, 2, 1), ((0, 0), (0, n_exp_pad - n_experts), (0, 0)))
    r_hi = router_t.astype(BF16)
    r_lo = (router_t - r_hi.astype(F32)).astype(BF16)
    router_cat = jnp.concatenate([r_hi, r_hi, r_lo], axis=-1)
    g_mix, g_ffn, g_ple = (g.reshape(depth, 1, d) for g in (norm_mix, norm_ffn, norm_ple))

    x = jnp.concatenate([x_prompt.reshape(bs, d), x_sample.reshape(ds, d), jnp.zeros((rows - bs - ds, d), F32)], axis=0)
    p_dim = p_prompt.shape[-1]
    p_all = jnp.concatenate([p_prompt.reshape(depth, bs, p_dim), p_sample.reshape(depth, ds, p_dim),
                             jnp.zeros((depth, rows - bs - ds, p_dim), F32)], axis=1)

    means_t = _block_means(cache_k, page_table).transpose(0, 1, 3, 2, 4)
    state_t = state_conv.transpose(0, 2, 1, 3)
    batch_ix = jnp.arange(db)[:, None, None, None, None]
    page_off = jnp.arange(ppb)

    bm_moe = 3 * ROW_TILE
    bm_dense = _row_tile(rows, 768)
    n_dense_chunks = rows // bm_dense
    dense_tables = (jnp.zeros((n_dense_chunks,), I32), jnp.arange(n_dense_chunks, dtype=I32),
                    jnp.full((n_dense_chunks,), bm_dense, I32))

    kp, vp, cp, ksm, vsm, csm = [], [], [], [], [], []
    y = None
    for i in range(depth):
        h = _inproj(x, g_mix, w_in_b, i)

        att_p, k_pages, v_pages = _moba_prompt(h, slopes, b, s, n_heads, hd, page)
        mix, u_tail = _mix_prompt(h, att_p, conv_w, i, b, s, d)

        hs = h[bs:bs + ds]
        qkv = hs[:, :3 * d].reshape(db, t_len, 3, n_heads, hd).transpose(2, 0, 3, 1, 4)
        qkv_p = jnp.pad(qkv, ((0, 0), (0, 0), (0, 0), (0, tp - t_len), (0, 0)))
        picks = _sample_select(qkv_p[0], means_t, i)[:, :, :t_len, :TOPK_BLOCKS]
        picks = picks.transpose(0, 2, 1, 3)
        cols = picks[..., None] * ppb + page_off
        phys = page_table[batch_ix, cols]
        att_s = _sample_attend(qkv_p[0], qkv_p[1], qkv_p[2], cache_k, cache_v, phys.reshape(-1).astype(I32),
                               picks.reshape(-1).astype(I32), slopes, i, t_len, past_len)
        att_s = att_s[:, :, :t_len].transpose(2, 0, 1, 3).reshape(t_len, db, d)
        fields_s = hs.reshape(db, t_len, N_IN_FIELDS, d).transpose(2, 1, 0, 3)
        mix_s, new_state = _mix_sample(fields_s, att_s, state_t[i], conv_w, i)
        mix = lax.dynamic_update_slice(mix, mix_s.transpose(1, 0, 2).reshape(ds, d), (bs, 0))

        if i % 2 == 0:
            x1, xn = _outproj(mix, x, w_out_b, g_ffn, i)
            x2 = _swiglu(xn, dw1, dw3, dw2, x1, *dense_tables, wsel=(i // 2,), bm=bm_dense, routed=False)
        else:
            x1, xn3, e_t, gate_t = _outproj(mix, x, w_out_b, g_ffn, i, router_cat[i // 2], n_experts)
            dest, tok_sorted, gate_sorted, sub_active, chunk_e, chunk_blk, chunk_rows = _route(
                e_t[:MOE_TOPK].T, gate_t[:MOE_TOPK].T, n_experts, bm_moe)
            xs3 = _gather_rows(xn3, tok_sorted, sub_active)
            ys3 = _swiglu(xs3, mw1, mw3, mw2, gate_sorted, chunk_e, chunk_blk, chunk_rows, wsel=(i // 2,), bm=bm_moe,
                          routed=True)
            x2 = _combine(x1, ys3, dest)

        x, y = _ple(x2, p_all, wg_b, wp_b, g_ple, i, norm_final.reshape(1, d) if i == depth - 1 else None)

        kp.append(k_pages); vp.append(v_pages)
        cp.append(u_tail[:, SUBLANE - (CONV_W - 1):, :])
        ksm.append(qkv[1]); vsm.append(qkv[2])
        csm.append(new_state.transpose(1, 0, 2))

    y_prompt = y[:bs].reshape(b, s, d)
    y_sample = y[bs:bs + ds].reshape(db, t_len, d)
    return (y_prompt, y_sample, jnp.stack(kp), jnp.stack(vp), jnp.stack(cp), jnp.stack(ksm), jnp.stack(vsm), jnp.stack(csm))
```

```python
import functools
import math

import jax
import jax.numpy as jnp
from jax import lax
from jax.experimental import pallas as pl
from jax.experimental.pallas import tpu as pltpu

F32 = jnp.float32
BF16 = jnp.bfloat16
I32 = jnp.int32

RMS_EPS = 1e-6
NEG = -1e30
BLOCK = 256
TOPK_BLOCKS = 3
MOE_TOPK = 2
CONV_W = 3
N_IN_FIELDS = 8
LOG2E = math.log2(math.e)

LANE = 128
SUBLANE = 8
ROW_TILE = 256
VMEM_LIMIT = 56 * 1024 * 1024
N_ALIBI_TERMS = 3


def _cparams(n_axes, vmem=VMEM_LIMIT):
    return pltpu.CompilerParams(dimension_semantics=("arbitrary",) * n_axes, vmem_limit_bytes=vmem)


def _row_tile(rows, max_tile):
    n = rows // ROW_TILE
    best = 1
    for k in range(1, n + 1):
        if n % k == 0 and k * ROW_TILE <= max_tile:
            best = k
    return best * ROW_TILE


def _col_tile(cols, max_tile):
    best = LANE
    for t in range(LANE, min(cols, max_tile) + 1, LANE):
        if cols % t == 0:
            best = t
    return best


def _rmsnorm(x, g):
    ms = jnp.mean(x * x, axis=-1, keepdims=True)
    return x * lax.rsqrt(ms + RMS_EPS) * g


def _dot_nt(a, b):
    return lax.dot_general(a, b, (((1,), (1,)), ((), ())), preferred_element_type=F32)


def _split_bf16(x, n=2):
    terms = []
    for _ in range(n - 1):
        t = x.astype(BF16)
        terms.append(t)
        x = x - t.astype(F32)
    terms.append(x.astype(BF16))
    return terms


def _dot3(x, w):
    (xh, xl), (wh, wl) = _split_bf16(x), _split_bf16(w)
    dot = functools.partial(jnp.dot, preferred_element_type=F32)
    return dot(xh, wh) + (dot(xl, wh) + dot(xh, wl))


def _dot_nt3(a, b):
    (ah, al), (bh, bl) = _split_bf16(a), _split_bf16(b)
    return _dot_nt(ah, bh) + (_dot_nt(al, bh) + _dot_nt(ah, bl))


def _block_of(pos):
    assert BLOCK & (BLOCK - 1) == 0
    return lax.shift_right_logical(pos, BLOCK.bit_length() - 1)


def _top_picks(cand, col, n, sentinel, axis=-1):
    picks = []
    colf = col.astype(F32)
    for _ in range(n):
        m = jnp.max(cand, axis=axis, keepdims=True)
        first = jnp.min(jnp.where(cand == m, colf, float(sentinel)), axis=axis, keepdims=True)
        picks.append(first.astype(I32))
        cand = jnp.where(colf == first, -jnp.inf, cand)
    return picks


def _load_row_slab(ref, lead, r0, n_rows, sl, n_sl):
    return ref[lead + (pl.ds(r0 * n_sl + sl, n_rows, stride=n_sl), slice(None))]


def _store_row_slabs(ref, r0, val):
    n_rows, d = val.shape
    n_sl = d // LANE
    for sl in range(n_sl):
        ref[pl.ds(r0 * n_sl + sl, n_rows, stride=n_sl), :] = val[:, sl * LANE:(sl + 1) * LANE]


def _norm_rows_kernel(x_ref, g_ref, o_ref):
    o_ref[...] = _rmsnorm(x_ref[...], g_ref[...]).astype(o_ref.dtype)


def _norm_rows(x, g, layer):
    rows, d = x.shape
    tm = _row_tile(rows, 768)
    return pl.pallas_call(
        _norm_rows_kernel,
        grid=(rows // tm,),
        in_specs=[pl.BlockSpec((tm, d), lambda i: (i, 0)), pl.BlockSpec((None, 1, d), lambda i: (layer, 0, 0))],
        out_specs=pl.BlockSpec((tm, d), lambda i: (i, 0)),
        out_shape=jax.ShapeDtypeStruct((rows, d), BF16),
        compiler_params=_cparams(1),
        name="norm_rows",
    )(x, g)


def _inproj_kernel(xn_ref, w_ref, o_ref):
    o_ref[...] = jnp.dot(xn_ref[...], w_ref[...].astype(BF16), preferred_element_type=F32)


def _inproj(xn, w, layer):
    rows, d = xn.shape
    n = w.shape[-1]
    tm = _row_tile(rows, 2816)
    tn = _col_tile(n, 512)
    return pl.pallas_call(
        _inproj_kernel,
        grid=(rows // tm, n // tn),
        in_specs=[
            pl.BlockSpec((tm, d), lambda i, j: (i, 0)),
            pl.BlockSpec((None, d, tn), lambda i, j: (layer, 0, j)),
        ],
        out_specs=pl.BlockSpec((tm, tn), lambda i, j: (i, j)),
        out_shape=jax.ShapeDtypeStruct((rows, n), F32),
        compiler_params=_cparams(2),
        name="inproj",
    )(xn, w)


def _moba_prompt_kernel(slopes_ref, q_ref, k_ref, v_ref, kall_ref, vall_ref, o_ref, ko_ref, vo_ref, kx_ref, vx_ref,
                        mt_ref, pen_ref, *, nb, hd, nbp, sel_rows):
    del kall_ref, vall_ref
    s_len = nb * BLOCK
    slope2 = slopes_ref[pl.program_id(1)] * LOG2E
    qscale = hd ** -0.5 * LOG2E
    k = k_ref[...]
    v = v_ref[...]
    ko_ref[...] = k.reshape(ko_ref.shape)
    vo_ref[...] = v.reshape(vo_ref.shape)

    lane = lax.broadcasted_iota(I32, (s_len, hd), 1)
    row = lax.broadcasted_iota(I32, (s_len, hd), 0)
    a_terms = _split_bf16(slope2 * row.astype(F32), N_ALIBI_TERMS)
    kext = jnp.where(lane == _block_of(row), 1.0, 0.0).astype(BF16)
    for t, a in enumerate(a_terms):
        kext = jnp.where(lane == nb + t, a, kext)
    kx_ref[...] = jnp.concatenate([k.astype(BF16), kext], axis=1)
    vx_ref[...] = jnp.concatenate([v.astype(BF16), jnp.where(lane == 0, 1.0, 0.0).astype(BF16)], axis=1)

    means = jnp.mean(k.reshape(nb, BLOCK, hd), axis=1)
    m_hi, m_lo = _split_bf16(means)
    mt_ref[...] = jnp.zeros_like(mt_ref)
    mt_ref[0:nb, :] = jnp.concatenate([m_hi, m_hi, m_lo], axis=1)

    col = lax.broadcasted_iota(I32, (sel_rows, nbp), 1)
    rsel = lax.broadcasted_iota(I32, (sel_rows, nbp), 0)
    ones_lane = (col >= nb) & (col < nb + N_ALIBI_TERMS)

    def select(c, carry):
        row0 = pl.multiple_of(c * sel_rows, sel_rows)
        q_hi, q_lo = _split_bf16(q_ref[pl.ds(row0, sel_rows), :])
        sc = _dot_nt(jnp.concatenate([q_hi, q_lo, q_hi], axis=1), mt_ref[...])
        n_past = _block_of(row0 + rsel)
        picks = _top_picks(jnp.where(col < n_past, sc, NEG), col, TOPK_BLOCKS, nbp)
        pen = jnp.where(col == n_past, 0.0, NEG)
        for r, p in enumerate(picks):
            pen = jnp.where((col == p) & (r < n_past), 0.0, pen)
        pen_ref[pl.ds(row0, sel_rows), :] = jnp.where(ones_lane, 1.0, pen).astype(BF16)
        return carry

    lax.fori_loop(0, s_len // sel_rows, select, 0)

    r_io = lax.broadcasted_iota(I32, (BLOCK, BLOCK), 0)
    c_io = lax.broadcasted_iota(I32, (BLOCK, BLOCK), 1)

    for i in range(nb):
        row0, win = i * BLOCK, i * BLOCK
        qx = jnp.concatenate([(q_ref[row0:row0 + BLOCK, :] * qscale).astype(BF16), pen_ref[row0:row0 + BLOCK, :]], axis=1)
        s_all = _dot_nt(qx, kx_ref[0:win + BLOCK, :])
        s_own = jnp.where(r_io >= c_io, s_all[:, win:], NEG)
        s_all = jnp.concatenate([s_all[:, :win], s_own], axis=1) if i else s_own
        m = jnp.max(s_all, axis=-1, keepdims=True)
        p = jnp.exp2(s_all - m).astype(BF16)
        o2 = jnp.dot(p, vx_ref[0:win + BLOCK, :], preferred_element_type=F32)
        o_ref[row0:row0 + BLOCK, :] = o2[:, :hd] / o2[:, hd:hd + 1]


def _moba_prompt(h, slopes, k_all, v_all, layer, b, s, n_heads, hd):
    assert s % BLOCK == 0 and TOPK_BLOCKS == 3
    nb = s // BLOCK
    nbp = hd
    assert nb + N_ALIBI_TERMS <= nbp
    sel_rows = min(s, 1024)
    assert s % sel_rows == 0
    n_pg, page = k_all.shape[2], k_all.shape[4]
    kern = functools.partial(_moba_prompt_kernel, nb=nb, hd=hd, nbp=nbp, sel_rows=sel_rows)
    pages = jax.ShapeDtypeStruct(k_all.shape, F32)
    page_spec = pl.BlockSpec((None, None, n_pg, None, page, hd), lambda bi, hi: (layer, bi, 0, hi, 0, 0))
    anyspec = pl.BlockSpec(memory_space=pl.ANY)
    return pl.pallas_call(
        kern,
        grid=(b, n_heads),
        in_specs=[
            pl.BlockSpec(memory_space=pltpu.SMEM),
            pl.BlockSpec((s, hd), lambda bi, hi: (bi, hi)),
            pl.BlockSpec((s, hd), lambda bi, hi: (bi, n_heads + hi)),
            pl.BlockSpec((s, hd), lambda bi, hi: (bi, 2 * n_heads + hi)),
            anyspec, anyspec,
        ],
        out_specs=[pl.BlockSpec((s, hd), lambda bi, hi: (bi, hi)), page_spec, page_spec],
        out_shape=[jax.ShapeDtypeStruct((b * s, n_heads * hd), F32), pages, pages],
        input_output_aliases={4: 1, 5: 2},
        scratch_shapes=[
            pltpu.VMEM((s, 2 * hd), BF16),
            pltpu.VMEM((s, 2 * hd), BF16),
            pltpu.VMEM((nbp, 3 * hd), BF16),
            pltpu.VMEM((s, nbp), BF16),
        ],
        compiler_params=_cparams(2),
        name="moba_prompt",
    )(slopes, h, h, h, k_all, v_all)


def _block_means_kernel(pt_ref, *refs, ppb, page, bps):
    del pt_ref
    o_ref = refs[ppb * bps]
    for blk in range(bps):
        tot = jnp.sum(refs[blk * ppb][0, 0], axis=1)
        for r in refs[blk * ppb + 1:(blk + 1) * ppb]:
            tot = tot + jnp.sum(r[0, 0], axis=1)
        o_ref[0, 0, blk] = tot * (1.0 / (ppb * page))


def _block_means(cache_k, page_table):
    depth, _, n_heads, page, hd = cache_k.shape
    db, n_pages = page_table.shape
    ppb = BLOCK // page
    nbk = n_pages // ppb
    bps = 2 if nbk % 2 == 0 else 1
    pt = page_table.reshape(-1)

    def page_map(jp, l, b, n, pt_ref):
        return (l, pt_ref[b * n_pages + n * (ppb * bps) + jp], 0, 0, 0)

    return pl.pallas_call(
        functools.partial(_block_means_kernel, ppb=ppb, page=page, bps=bps),
        grid_spec=pltpu.PrefetchScalarGridSpec(
            num_scalar_prefetch=1,
            grid=(depth, db, nbk // bps),
            in_specs=[pl.BlockSpec((1, 1, n_heads, page, hd), functools.partial(page_map, jp))
                      for jp in range(ppb * bps)],
            out_specs=pl.BlockSpec((1, 1, bps, n_heads, hd), lambda l, b, n, pt_ref: (l, b, n, 0, 0)),
        ),
        out_shape=jax.ShapeDtypeStruct((depth, db, nbk, n_heads, hd), F32),
        compiler_params=_cparams(3),
        name="block_means",
    )(pt, *([cache_k] * (ppb * bps)))


def _sample_select_kernel(q_ref, m_ref, o_ref, *, nbk):
    q = q_ref[0]
    means = m_ref[0, 0]
    sc = lax.dot_general(q, means, (((2,), (2,)), ((0,), (0,))),
                         precision=lax.Precision.HIGHEST, preferred_element_type=F32)
    col = lax.broadcasted_iota(I32, sc.shape, 2)
    picks = _top_picks(sc, col, TOPK_BLOCKS, nbk)
    lane = lax.broadcasted_iota(I32, o_ref.shape[1:], 2)
    o_ref[0] = jnp.where(lane == 0, picks[0], jnp.where(lane == 1, picks[1], picks[2]))


def _sample_select(q_s, means_t, layer):
    db, n_heads, tp, hd = q_s.shape
    nbk = means_t.shape[3]
    assert nbk >= TOPK_BLOCKS
    return pl.pallas_call(
        functools.partial(_sample_select_kernel, nbk=nbk),
        grid=(db,),
        in_specs=[
            pl.BlockSpec((1, n_heads, tp, hd), lambda b: (b, 0, 0, 0)),
            pl.BlockSpec((1, 1, n_heads, nbk, hd), lambda b: (layer, b, 0, 0, 0)),
        ],
        out_specs=pl.BlockSpec((1, n_heads, tp, LANE), lambda b: (b, 0, 0, 0)),
        out_shape=jax.ShapeDtypeStruct((db, n_heads, tp, LANE), I32),
        compiler_params=_cparams(1),
        name="sample_select",
    )(q_s, means_t)


def _sample_attend_kernel(phys_ref, blk_ref, slopes_ref, q_ref, kn_ref, vn_ref, ck_hbm, cv_hbm, o_ref, kbuf, vbuf, sems,
                          *, layer, t_len, n_sel, past_len, hd, page):
    b, hh = pl.program_id(0), pl.program_id(1)
    n_heads = pl.num_programs(1)
    step = b * n_heads + hh
    n_steps = pl.num_programs(0) * n_heads

    def page_copies(st, slot):
        sb, sh = st // n_heads, st % n_heads
        cps = []
        for t in range(t_len):
            for c in range(n_sel):
                pg = phys_ref[((sb * t_len + t) * n_heads + sh) * n_sel + c]
                j = t * n_sel + c
                cps.append(pltpu.make_async_copy(ck_hbm.at[layer, pg, sh], kbuf.at[slot, j], sems.at[0, slot]))
                cps.append(pltpu.make_async_copy(cv_hbm.at[layer, pg, sh], vbuf.at[slot, j], sems.at[1, slot]))
        return cps

    @pl.when(step == 0)
    def _():
        for cp in page_copies(step, 0):
            cp.start()

    @pl.when(step + 1 < n_steps)
    def _():
        for cp in page_copies(step + 1, (step + 1) % 2):
            cp.start()

    slot = step % 2
    for cp in page_copies(step, slot):
        cp.wait()

    slope = slopes_ref[hh]
    scale = hd ** -0.5
    ppb = BLOCK // page
    qf = q_ref[0, 0] * scale
    tp = qf.shape[0]
    s_own_all = _dot_nt3(qf, kn_ref[0, 0])
    vn = vn_ref[0, 0]
    lane = lax.broadcasted_iota(I32, (1, page), 1)
    tcol = lax.broadcasted_iota(I32, (1, tp), 1)
    outs = []
    for t in range(t_len):
        tq = past_len + t
        pos = jnp.concatenate(
            [blk_ref[((b * t_len + t) * n_heads + hh) * TOPK_BLOCKS + c // ppb] * BLOCK + (c % ppb) * page + lane
             for c in range(n_sel)], axis=1)
        k_sel = kbuf[slot, t * n_sel:(t + 1) * n_sel].reshape(n_sel * page, hd)
        v_sel = vbuf[slot, t * n_sel:(t + 1) * n_sel].reshape(n_sel * page, hd)
        s_sel = _dot_nt3(qf, k_sel)[t:t + 1, :] - slope * (tq - pos).astype(F32)
        s_own = s_own_all[t:t + 1, :] - slope * (t - tcol).astype(F32)
        s_own = jnp.where(tcol <= t, s_own, NEG)
        m = jnp.maximum(jnp.max(s_own, axis=-1, keepdims=True), jnp.max(s_sel, axis=-1, keepdims=True))
        p_own = jnp.exp(s_own - m)
        p_sel = jnp.exp(s_sel - m)
        l = jnp.sum(p_own, axis=-1, keepdims=True) + jnp.sum(p_sel, axis=-1, keepdims=True)
        acc = _dot3(jnp.broadcast_to(p_own, (tp, tp)), vn) + _dot3(jnp.broadcast_to(p_sel, (tp, n_sel * page)), v_sel)
        outs.append(acc[0:1, :] / l)
    outs += [jnp.zeros((1, hd), F32)] * (tp - t_len)
    o_ref[0, 0] = jnp.concatenate(outs, axis=0)


def _sample_attend(q_s, k_new, v_new, cache_k, cache_v, phys, blk, slopes, layer, t_len, past_len):
    db, n_heads, tp, hd = q_s.shape
    page = cache_k.shape[3]
    n_sel = TOPK_BLOCKS * (BLOCK // page)
    n_pg = t_len * n_sel
    small = pl.BlockSpec((1, 1, tp, hd), lambda b, hh, phys_ref, blk_ref: (b, hh, 0, 0))
    anyspec = pl.BlockSpec(memory_space=pl.ANY)
    kern = functools.partial(_sample_attend_kernel, layer=layer, t_len=t_len, n_sel=n_sel, past_len=past_len, hd=hd,
                             page=page)
    return pl.pallas_call(
        kern,
        grid_spec=pltpu.PrefetchScalarGridSpec(
            num_scalar_prefetch=2,
            grid=(db, n_heads),
            in_specs=[pl.BlockSpec(memory_space=pltpu.SMEM), small, small, small, anyspec, anyspec],
            out_specs=small,
            scratch_shapes=[pltpu.VMEM((2, n_pg, page, hd), F32), pltpu.VMEM((2, n_pg, page, hd), F32),
                            pltpu.SemaphoreType.DMA((2, 2))],
        ),
        out_shape=jax.ShapeDtypeStruct((db, n_heads, tp, hd), F32),
        compiler_params=_cparams(2),
        name="sample_attend",
    )(phys, blk, slopes, q_s, k_new, v_new, cache_k, cache_v)


def _mix(gb, ga, gcv, att, u0, u1, u2, cw):
    conv = cw[0:1] * u0 + cw[1:2] * u1 + cw[2:3] * u2
    return jax.nn.sigmoid(ga) * att + jax.nn.sigmoid(gcv) * (gb * conv)


def _mix_prompt_kernel(gb_ref, gc_ref, xc_ref, ga_ref, gcv_ref, gch_ref, xch_ref, att_ref, cw_ref, o_ref, us_ref,
                       *, tiles_per_seq, n_tiles):
    i = pl.program_id(0)

    @pl.when(i >= n_tiles)
    def _():
        o_ref[...] = jnp.zeros_like(o_ref)

    @pl.when(i < n_tiles)
    def _():
        cw = cw_ref[...]
        u = gc_ref[...] * xc_ref[...]
        halo = jnp.where(i % tiles_per_seq == 0, 0.0, gch_ref[...] * xch_ref[...])
        u1 = pltpu.roll(u, 1, 0)
        u2 = pltpu.roll(u, 2, 0)
        o_ref[...] = _mix(gb_ref[...], ga_ref[...], gcv_ref[...], att_ref[...], u2, u1, u, cw).astype(o_ref.dtype)
        row = lax.broadcasted_iota(I32, (SUBLANE, u.shape[1]), 0)
        f1 = jnp.where(row < 1, pltpu.roll(halo, 1, 0), u1[0:SUBLANE])
        f2 = jnp.where(row < 2, pltpu.roll(halo, 2, 0), u2[0:SUBLANE])
        o_ref[0:SUBLANE, :] = _mix(gb_ref[0:SUBLANE, :], ga_ref[0:SUBLANE, :], gcv_ref[0:SUBLANE, :],
                                   att_ref[0:SUBLANE, :], f2, f1, u[0:SUBLANE], cw).astype(o_ref.dtype)
        us_ref[...] = u[u.shape[0] - SUBLANE:, :]


def _mix_prompt(h, att, conv_w, layer, b, s, d):
    assert CONV_W == 3
    tm = ROW_TILE
    tps = s // tm
    hpt = tm // SUBLANE
    n_tiles = b * tps
    rows = h.shape[0]
    clamp = lambda i: jnp.minimum(i, n_tiles - 1)

    def field(f):
        return pl.BlockSpec((tm, d), lambda i: (clamp(i), f))

    def halo(f):
        return pl.BlockSpec((SUBLANE, d), lambda i: (jnp.maximum(clamp(i) * hpt - 1, 0), f))

    return pl.pallas_call(
        functools.partial(_mix_prompt_kernel, tiles_per_seq=tps, n_tiles=n_tiles),
        grid=(rows // tm,),
        in_specs=[field(3), field(4), field(5), field(6), field(7), halo(4), halo(5),
                  pl.BlockSpec((tm, d), lambda i: (clamp(i), 0)),
                  pl.BlockSpec((None, CONV_W, d), lambda i: (layer, 0, 0))],
        out_specs=[pl.BlockSpec((tm, d), lambda i: (i, 0)),
                   pl.BlockSpec((None, SUBLANE, d), lambda i: (clamp(i) // tps, 0, 0))],
        out_shape=[jax.ShapeDtypeStruct((rows, d), BF16), jax.ShapeDtypeStruct((b, SUBLANE, d), F32)],
        compiler_params=_cparams(1),
        name="mix_prompt",
    )(h, h, h, h, h, h, h, att, conv_w)


def _mix_sample_kernel(f_ref, att_ref, st_ref, cw_ref, o_ref, ns_ref, *, t_len):
    cw = cw_ref[...]
    u = [st_ref[0], st_ref[1]] + [f_ref[4, t] * f_ref[5, t] for t in range(t_len)]
    for t in range(t_len):
        o_ref[t] = _mix(f_ref[3, t], f_ref[6, t], f_ref[7, t], att_ref[t], u[t], u[t + 1], u[t + 2], cw).astype(o_ref.dtype)
    ns_ref[0] = u[t_len]
    ns_ref[1] = u[t_len + 1]


def _mix_sample(fields, att, state, conv_w, layer):
    _, t_len, db, d = fields.shape
    return pl.pallas_call(
        functools.partial(_mix_sample_kernel, t_len=t_len),
        grid=(1,),
        in_specs=[pl.BlockSpec(fields.shape, lambda i: (0, 0, 0, 0)),
                  pl.BlockSpec(att.shape, lambda i: (0, 0, 0)),
                  pl.BlockSpec(state.shape, lambda i: (0, 0, 0)),
                  pl.BlockSpec((None, CONV_W, d), lambda i: (layer, 0, 0))],
        out_specs=[pl.BlockSpec((t_len, db, d), lambda i: (0, 0, 0)),
                   pl.BlockSpec((CONV_W - 1, db, d), lambda i: (0, 0, 0))],
        out_shape=[jax.ShapeDtypeStruct((t_len, db, d), F32), jax.ShapeDtypeStruct((CONV_W - 1, db, d), F32)],
        compiler_params=_cparams(1),
        name="mix_sample",
    )(fields, att, state, conv_w)


def _outproj_kernel(*refs, n_experts):
    if n_experts:
        mix_ref, x_ref, w_ref, g_ref, r_ref, x1_ref, xnf_ref, e_ref, gt_ref = refs
    else:
        mix_ref, x_ref, w_ref, g_ref, x1_ref, xn_ref = refs
    x1 = x_ref[...] + jnp.dot(mix_ref[...], w_ref[...], preferred_element_type=F32)
    x1_ref[...] = x1
    xn = _rmsnorm(x1, g_ref[...])
    if not n_experts:
        xn_ref[...] = xn.astype(BF16)
        return
    _store_row_slabs(xnf_ref, 0, xn)
    x_hi, x_lo = _split_bf16(xn)
    logits = _dot_nt(r_ref[...], jnp.concatenate([x_hi, x_lo, x_hi], axis=1))
    ex = lax.broadcasted_iota(I32, logits.shape, 0)
    logits = jnp.where(ex < n_experts, logits, -jnp.inf)
    (i1, i2) = _top_picks(logits, ex, MOE_TOPK, logits.shape[0], axis=0)
    m1 = jnp.max(logits, axis=0, keepdims=True)
    m2 = jnp.max(jnp.where(ex == i1, -jnp.inf, logits), axis=0, keepdims=True)
    e2 = jnp.exp(m2 - m1)
    den = 1.0 + e2
    orow = lax.broadcasted_iota(I32, e_ref.shape, 0)
    e_ref[...] = jnp.where(orow == 0, i1, i2)
    gt_ref[...] = jnp.where(orow == 0, 1.0 / den, e2 / den)


def _outproj(mix, x, w, g, layer, router_t=None, n_experts=0):
    rows, d = x.shape
    tm = _row_tile(rows, 256)
    rowspec = pl.BlockSpec((tm, d), lambda i: (i, 0))
    in_specs = [rowspec, rowspec,
                pl.BlockSpec((None, d, d), lambda i: (layer, 0, 0), pipeline_mode=pl.Buffered(1)),
                pl.BlockSpec((None, 1, d), lambda i: (layer, 0, 0))]
    args = [mix, x, w, g]
    if n_experts:
        tspec = pl.BlockSpec((SUBLANE, tm), lambda i: (0, i))
        in_specs.append(pl.BlockSpec(router_t.shape, lambda i: (0, 0)))
        out_specs = [rowspec, pl.BlockSpec((tm * (d // LANE), LANE), lambda i: (i, 0)), tspec, tspec]
        out_shape = [jax.ShapeDtypeStruct((rows, d), F32), jax.ShapeDtypeStruct((rows * (d // LANE), LANE), F32),
                     jax.ShapeDtypeStruct((SUBLANE, rows), I32), jax.ShapeDtypeStruct((SUBLANE, rows), F32)]
        args.append(router_t)
    else:
        out_specs = [rowspec, rowspec]
        out_shape = [jax.ShapeDtypeStruct((rows, d), F32), jax.ShapeDtypeStruct((rows, d), BF16)]
    return pl.pallas_call(
        functools.partial(_outproj_kernel, n_experts=n_experts),
        grid=(rows // tm,),
        in_specs=in_specs, out_specs=out_specs, out_shape=out_shape,
        compiler_params=_cparams(1),
        name="outproj",
    )(*args)


def _swiglu_kernel(ce_ref, cb_ref, cr_ref, x_ref, w1_ref, w3_ref, w2_ref, s_ref, o_ref, acc_ref, *xb_scratch,
                   n_f, bm, routed):
    del ce_ref, cb_ref
    c, f = pl.program_id(0), pl.program_id(1)
    rows = cr_ref[c]
    nsub = (rows + (ROW_TILE - 1)) // ROW_TILE
    n_sl = acc_ref.shape[1] // LANE

    if routed:
        xb_ref, = xb_scratch

        @pl.when(f == 0)
        def _():
            def conv(sb, carry):
                r0 = pl.multiple_of(sb * ROW_TILE, ROW_TILE)
                for sl in range(n_sl):
                    xb_ref[pl.ds(r0, ROW_TILE), sl * LANE:(sl + 1) * LANE] = _load_row_slab(
                        x_ref, (), r0, ROW_TILE, sl, n_sl).astype(BF16)
                return carry

            lax.fori_loop(0, nsub, conv, 0)
    else:
        xb_ref = x_ref

    def accumulate(r0, n_rows):
        xs = xb_ref[pl.ds(r0, n_rows), :]
        h1 = jnp.dot(xs, w1_ref[...], preferred_element_type=F32)
        h3 = jnp.dot(xs, w3_ref[...], preferred_element_type=F32)
        hh = (jax.nn.silu(h1) * h3).astype(BF16)
        part = jnp.dot(hh, w2_ref[...], preferred_element_type=F32)

        @pl.when(f == 0)
        def _():
            acc_ref[pl.ds(r0, n_rows), :] = part

        @pl.when(f > 0)
        def _():
            acc_ref[pl.ds(r0, n_rows), :] += part

    @pl.when(nsub == bm // ROW_TILE)
    def _():
        accumulate(0, bm)

    @pl.when(nsub < bm // ROW_TILE)
    def _():
        def sub(sb, carry):
            accumulate(pl.multiple_of(sb * ROW_TILE, ROW_TILE), ROW_TILE)
            return carry

        lax.fori_loop(0, nsub, sub, 0)

    @pl.when(f == n_f - 1)
    def _():
        def fin(sb, carry):
            r0 = pl.multiple_of(sb * ROW_TILE, ROW_TILE)
            a = acc_ref[pl.ds(r0, ROW_TILE), :]
            if routed:
                _store_row_slabs(o_ref, r0, a * s_ref[pl.ds(r0, ROW_TILE), :])
            else:
                o_ref[pl.ds(r0, ROW_TILE), :] = s_ref[pl.ds(r0, ROW_TILE), :] + a
            return carry

        lax.fori_loop(0, nsub, fin, 0)

        def blank(sb, carry):
            rep = o_ref.shape[0] // bm
            r0 = pl.multiple_of(sb * ROW_TILE * rep, ROW_TILE * rep)
            o_ref[pl.ds(r0, ROW_TILE * rep), :] = jnp.zeros((ROW_TILE * rep, o_ref.shape[1]), F32)
            return carry

        lax.fori_loop(nsub, bm // ROW_TILE, blank, 0)


def _swiglu(x, w1, w3, w2, side, chunk_e, chunk_blk, chunk_rows, wsel, bm, routed):
    d = w1.shape[-2]
    ff = w1.shape[-1]
    tf = _col_tile(ff, 1024 if routed else 512)
    n_f = ff // tf
    n_chunks = chunk_e.shape[0]
    rep = d // LANE if routed else 1
    lead = len(wsel)
    per_expert = w1.ndim == lead + 3

    def fidx(c, f, cr):
        return jnp.where(cr[c] > 0, f, n_f - 1)

    def w13_map(c, f, ce, cb, cr):
        return wsel + ((ce[c],) if per_expert else ()) + (0, fidx(c, f, cr))

    def w2_map(c, f, ce, cb, cr):
        return wsel + ((ce[c],) if per_expert else ()) + (fidx(c, f, cr), 0)

    nlead = lead + (1 if per_expert else 0)
    w13_spec = pl.BlockSpec((None,) * nlead + (d, tf), w13_map)
    w2_spec = pl.BlockSpec((None,) * nlead + (tf, d), w2_map)
    once = dict(pipeline_mode=pl.Buffered(1))
    x_spec = pl.BlockSpec((bm * rep, x.shape[1]), lambda c, f, ce, cb, cr: (cb[c], 0), **(once if routed else {}))
    o_spec = pl.BlockSpec((bm * rep, x.shape[1]), lambda c, f, ce, cb, cr: (c, 0), **once)
    side_spec = pl.BlockSpec((bm, side.shape[1]), lambda c, f, ce, cb, cr: (cb[c], 0), **({} if routed else once))
    scratch = [pltpu.VMEM((bm, d), F32)] + ([pltpu.VMEM((bm, d), BF16)] if routed else [])
    return pl.pallas_call(
        functools.partial(_swiglu_kernel, n_f=n_f, bm=bm, routed=routed),
        grid_spec=pltpu.PrefetchScalarGridSpec(
            num_scalar_prefetch=3,
            grid=(n_chunks, n_f),
            in_specs=[x_spec, w13_spec, w13_spec, w2_spec, side_spec],
            out_specs=o_spec,
            scratch_shapes=scratch,
        ),
        out_shape=jax.ShapeDtypeStruct(x.shape, F32),
        compiler_params=_cparams(2),
        name="swiglu",
    )(chunk_e, chunk_blk, chunk_rows, x, w1, w3, w2, side)


def _gather_rows_kernel(tok_ref, act_ref, x_hbm, o_ref, sem, *, n_sl):
    sb = pl.program_id(0)
    base = sb * ROW_TILE

    def row(ref, r):
        return ref.at[pl.ds(pl.multiple_of(r * n_sl, n_sl), n_sl)]

    @pl.when(act_ref[sb] > 0)
    def _():
        def start(r, carry):
            pltpu.make_async_copy(row(x_hbm, tok_ref[base + r]), row(o_ref, r), sem).start()
            return carry

        lax.fori_loop(0, ROW_TILE, start, 0)

        def wait(r, carry):
            pltpu.make_async_copy(row(x_hbm, 0), row(o_ref, r), sem).wait()
            return carry

        lax.fori_loop(0, ROW_TILE, wait, 0)

    @pl.when(act_ref[sb] == 0)
    def _():
        o_ref[...] = jnp.zeros_like(o_ref)


def _gather_rows(x, tok_sorted, sub_active, n_sl):
    n_out = tok_sorted.shape[0]
    return pl.pallas_call(
        functools.partial(_gather_rows_kernel, n_sl=n_sl),
        grid_spec=pltpu.PrefetchScalarGridSpec(
            num_scalar_prefetch=2,
            grid=(n_out // ROW_TILE,),
            in_specs=[pl.BlockSpec(memory_space=pl.ANY)],
            out_specs=pl.BlockSpec((ROW_TILE * n_sl, x.shape[1]), lambda sb, tok_ref, act_ref: (sb, 0)),
            scratch_shapes=[pltpu.SemaphoreType.DMA(())],
        ),
        out_shape=jax.ShapeDtypeStruct((n_out * n_sl, x.shape[1]), x.dtype),
        compiler_params=_cparams(1),
        name="gather_rows",
    )(tok_sorted, sub_active, x)


def _combine_kernel(dest_ref, x_ref, y_hbm, o_ref, buf_ref, sem):
    i = pl.program_id(0)
    tm, d = x_ref.shape
    n_sl = d // LANE
    base = i * tm * MOE_TOPK

    def row(ref, r):
        return ref.at[pl.ds(pl.multiple_of(r * n_sl, n_sl), n_sl)]

    def start(r, carry):
        for k in range(MOE_TOPK):
            pltpu.make_async_copy(row(y_hbm, dest_ref[base + r * MOE_TOPK + k]), row(buf_ref.at[k], r), sem).start()
        return carry

    lax.fori_loop(0, tm, start, 0)

    def wait(r, carry):
        for k in range(MOE_TOPK):
            pltpu.make_async_copy(row(y_hbm, 0), row(buf_ref.at[k], r), sem).wait()
        return carry

    lax.fori_loop(0, tm, wait, 0)
    for sl in range(n_sl):
        f = _load_row_slab(buf_ref, (0,), 0, tm, sl, n_sl)
        for k in range(1, MOE_TOPK):
            f = f + _load_row_slab(buf_ref, (k,), 0, tm, sl, n_sl)
        o_ref[:, sl * LANE:(sl + 1) * LANE] = x_ref[:, sl * LANE:(sl + 1) * LANE] + f


def _combine(x, y, dest):
    rows, d = x.shape
    tm = ROW_TILE
    rowspec = pl.BlockSpec((tm, d), lambda i, dest_ref: (i, 0))
    return pl.pallas_call(
        _combine_kernel,
        grid_spec=pltpu.PrefetchScalarGridSpec(
            num_scalar_prefetch=1,
            grid=(rows // tm,),
            in_specs=[rowspec, pl.BlockSpec(memory_space=pl.ANY)],
            out_specs=rowspec,
            scratch_shapes=[pltpu.VMEM((MOE_TOPK, tm * (d // LANE), y.shape[1]), F32), pltpu.SemaphoreType.DMA(())],
        ),
        out_shape=jax.ShapeDtypeStruct(x.shape, F32),
        compiler_params=_cparams(1),
        name="combine",
    )(dest, x, y)


def _route(e_idx, gates, n_experts, bm):
    rows = e_idx.shape[0]
    ns = rows * MOE_TOPK
    n_chunks = ns // bm + n_experts
    flat_e = e_idx.reshape(-1)
    onehot = (flat_e[:, None] == jnp.arange(n_experts, dtype=I32)[None, :]).astype(I32)
    csum = jnp.cumsum(onehot, axis=0)
    rank = jnp.take_along_axis(csum, flat_e[:, None], axis=1)[:, 0] - 1
    counts = csum[-1]
    padded = (counts + bm - 1) // bm * bm
    end_pad = jnp.cumsum(padded)
    start_pad = end_pad - padded
    dest = (start_pad[flat_e] + rank).astype(I32)
    n_used = end_pad[-1] // bm
    cidx = jnp.minimum(jnp.arange(n_chunks, dtype=I32), n_used - 1)
    chunk_e = jnp.minimum(jnp.sum((end_pad[None, :] <= (cidx * bm)[:, None]).astype(I32), axis=1), n_experts - 1)
    rows_left = counts[chunk_e] - (cidx * bm - start_pad[chunk_e])
    chunk_rows = jnp.where(jnp.arange(n_chunks) < n_used, jnp.clip(rows_left, 0, bm), 0).astype(I32)
    n_buf = n_chunks * bm
    tok_sorted = jnp.zeros((n_buf,), I32).at[dest].set(jnp.arange(ns, dtype=I32) // MOE_TOPK)
    gate_sorted = jnp.zeros((n_buf,), F32).at[dest].set(gates.reshape(-1))
    sub_per = bm // ROW_TILE
    sub_in_chunk = jnp.arange(n_chunks * sub_per, dtype=I32) % sub_per
    sub_active = (sub_in_chunk * ROW_TILE < jnp.repeat(chunk_rows, sub_per)).astype(I32)
    return dest, tok_sorted, gate_sorted[:, None], sub_active, chunk_e.astype(I32), cidx.astype(I32), chunk_rows


def _ple_kernel(x_ref, p_ref, wg_ref, wp_ref, g_ref, gn_ref, o_ref, n_ref):
    x = x_ref[...]
    z = jnp.dot(_rmsnorm(x, g_ref[...]).astype(BF16), wg_ref[...], preferred_element_type=F32)
    pp = jnp.dot(p_ref[...].astype(BF16), wp_ref[...], preferred_element_type=F32)
    x3 = x + jax.nn.sigmoid(z) * pp
    o_ref[...] = x3
    n_ref[...] = _rmsnorm(x3, gn_ref[...]).astype(n_ref.dtype)


def _ple(x, p, wg, wp, g, layer, g_next, next_dtype):
    rows, d = x.shape
    pd = p.shape[-1]
    tm = _row_tile(rows, 256)
    rowspec = pl.BlockSpec((tm, d), lambda i: (i, 0))
    return pl.pallas_call(
        _ple_kernel,
        grid=(rows // tm,),
        in_specs=[rowspec,
                  pl.BlockSpec((None, tm, pd), lambda i: (layer, i, 0)),
                  pl.BlockSpec((None, d, d), lambda i: (layer, 0, 0), pipeline_mode=pl.Buffered(1)),
                  pl.BlockSpec((None, pd, d), lambda i: (layer, 0, 0), pipeline_mode=pl.Buffered(1)),
                  pl.BlockSpec((None, 1, d), lambda i: (layer, 0, 0)),
                  pl.BlockSpec((1, d), lambda i: (0, 0))],
        out_specs=[rowspec, rowspec],
        out_shape=[jax.ShapeDtypeStruct((rows, d), F32), jax.ShapeDtypeStruct((rows, d), next_dtype)],
        compiler_params=_cparams(1),
        name="ple",
    )(x, p, wg, wp, g, g_next)


def _s_inproj_kernel(x_ref, g_ref, w_ref, o_ref):
    o_ref[...] = _dot3(_rmsnorm(x_ref[...], g_ref[...]), w_ref[...])


def _s_inproj(x, g, w, layer):
    n_rows, d = x.shape
    n = w.shape[-1]
    tn = _col_tile(n, 1024)
    return pl.pallas_call(
        _s_inproj_kernel,
        grid=(n // tn,),
        in_specs=[pl.BlockSpec((n_rows, d), lambda j: (0, 0)),
                  pl.BlockSpec((None, 1, d), lambda j: (layer, 0, 0)),
                  pl.BlockSpec((None, d, tn), lambda j: (layer, 0, j))],
        out_specs=pl.BlockSpec((n_rows, tn), lambda j: (0, j)),
        out_shape=jax.ShapeDtypeStruct((n_rows, n), F32),
        compiler_params=_cparams(1),
        name="s_inproj",
    )(x, g, w)


def _s_outproj_kernel(*refs, n_experts):
    if n_experts:
        mix_ref, x_ref, w_ref, g_ref, r_ref, x1_ref, xn_ref, e_ref, gt_ref = refs
    else:
        mix_ref, x_ref, w_ref, g_ref, x1_ref, xn_ref = refs
    x1 = x_ref[...] + _dot3(mix_ref[...], w_ref[...])
    x1_ref[...] = x1
    xn = _rmsnorm(x1, g_ref[...])
    xn_ref[...] = xn
    if n_experts:
        logits = _dot_nt3(r_ref[...], xn)
        ex = lax.broadcasted_iota(I32, logits.shape, 0)
        logits = jnp.where(ex < n_experts, logits, -jnp.inf)
        (i1, i2) = _top_picks(logits, ex, MOE_TOPK, logits.shape[0], axis=0)
        m1 = jnp.max(logits, axis=0, keepdims=True)
        m2 = jnp.max(jnp.where(ex == i1, -jnp.inf, logits), axis=0, keepdims=True)
        e2 = jnp.exp(m2 - m1)
        den = 1.0 + e2
        orow = lax.broadcasted_iota(I32, e_ref.shape, 0)
        e_ref[...] = jnp.where(orow == 0, i1, i2)
        gt_ref[...] = jnp.where(orow == 0, 1.0 / den, e2 / den)


def _s_outproj(mix, x, w, g, layer, router_t=None, n_experts=0):
    n_rows, d = x.shape
    full = pl.BlockSpec((n_rows, d), lambda i: (0, 0))
    in_specs = [full, full,
                pl.BlockSpec((None, d, d), lambda i: (layer, 0, 0), pipeline_mode=pl.Buffered(1)),
                pl.BlockSpec((None, 1, d), lambda i: (layer, 0, 0))]
    out_specs = [full, full]
    out_shape = [jax.ShapeDtypeStruct((n_rows, d), F32), jax.ShapeDtypeStruct((n_rows, d), F32)]
    args = [mix, x, w, g]
    if n_experts:
        tspec = pl.BlockSpec((SUBLANE, n_rows), lambda i: (0, 0))
        in_specs.append(pl.BlockSpec(router_t.shape, lambda i: (0, 0)))
        out_specs += [tspec, tspec]
        out_shape += [jax.ShapeDtypeStruct((SUBLANE, n_rows), I32), jax.ShapeDtypeStruct((SUBLANE, n_rows), F32)]
        args.append(router_t)
    return pl.pallas_call(
        functools.partial(_s_outproj_kernel, n_experts=n_experts),
        grid=(1,),
        in_specs=in_specs, out_specs=out_specs, out_shape=out_shape,
        compiler_params=_cparams(1),
        name="s_outproj",
    )(*args)


def _s_swiglu_kernel(x_ref, w1_ref, w3_ref, w2_ref, gate_ref, res_ref, o_ref, acc_ref):
    e, f = pl.program_id(0), pl.program_id(1)

    @pl.when((e == 0) & (f == 0))
    def _():
        acc_ref[...] = jnp.zeros_like(acc_ref)

    xs = x_ref[...]
    hh = jax.nn.silu(_dot3(xs, w1_ref[...])) * _dot3(xs, w3_ref[...])
    gate = gate_ref[...]
    acc_ref[...] += jnp.where(gate != 0.0, gate * _dot3(hh, w2_ref[...]), 0.0)

    @pl.when((e == pl.num_programs(0) - 1) & (f == pl.num_programs(1) - 1))
    def _():
        o_ref[...] = res_ref[...] + acc_ref[...]


def _s_swiglu(x, w1, w3, w2, gates, res, wsel):
    n_rows, d = x.shape
    ff = w1.shape[-1]
    tf = _col_tile(ff, 512)
    n_e = gates.shape[0]
    per_expert = w1.ndim == len(wsel) + 3
    lead = (None,) * (len(wsel) + (1 if per_expert else 0))
    esel = (lambda e: (e,)) if per_expert else (lambda e: ())
    full = pl.BlockSpec((n_rows, d), lambda e, f: (0, 0))
    w13_spec = pl.BlockSpec(lead + (d, tf), lambda e, f: wsel + esel(e) + (0, f))
    w2_spec = pl.BlockSpec(lead + (tf, d), lambda e, f: wsel + esel(e) + (f, 0))
    return pl.pallas_call(
        _s_swiglu_kernel,
        grid=(n_e, ff // tf),
        in_specs=[full, w13_spec, w13_spec, w2_spec, pl.BlockSpec((None, n_rows, 1), lambda e, f: (e, 0, 0)), full],
        out_specs=full,
        out_shape=jax.ShapeDtypeStruct((n_rows, d), F32),
        scratch_shapes=[pltpu.VMEM((n_rows, d), F32)],
        compiler_params=_cparams(2),
        name="s_swiglu",
    )(x, w1, w3, w2, gates, res)


def _s_ple_kernel(x_ref, p_ref, wg_ref, wp_ref, g_ref, gf_ref, o_ref, y_ref):
    x = x_ref[...]
    z = _dot3(_rmsnorm(x, g_ref[...]), wg_ref[...])
    x3 = x + jax.nn.sigmoid(z) * _dot3(p_ref[...], wp_ref[...])
    o_ref[...] = x3
    y_ref[...] = _rmsnorm(x3, gf_ref[...])


def _s_ple(x, p, wg, wp, g, layer, g_final):
    n_rows, d = x.shape
    pd = p.shape[-1]
    full = pl.BlockSpec((n_rows, d), lambda i: (0, 0))
    return pl.pallas_call(
        _s_ple_kernel,
        grid=(1,),
        in_specs=[full,
                  pl.BlockSpec((None, n_rows, pd), lambda i: (layer, 0, 0)),
                  pl.BlockSpec((None, d, d), lambda i: (layer, 0, 0), pipeline_mode=pl.Buffered(1)),
                  pl.BlockSpec((None, pd, d), lambda i: (layer, 0, 0), pipeline_mode=pl.Buffered(1)),
                  pl.BlockSpec((None, 1, d), lambda i: (layer, 0, 0)),
                  pl.BlockSpec((1, d), lambda i: (0, 0))],
        out_specs=[full, full],
        out_shape=[jax.ShapeDtypeStruct((n_rows, d), F32), jax.ShapeDtypeStruct((n_rows, d), F32)],
        compiler_params=_cparams(1),
        name="s_ple",
    )(x, p, wg, wp, g, g_final)


def kernel(x_prompt, x_sample, cache_k, cache_v, state_conv, page_table, p_prompt, p_sample, norm_mix, w_in, conv_w,
           w_out, norm_ffn, dense_w1, dense_w3, dense_w2, moe_router, moe_w1, moe_w3, moe_w2, norm_ple, w_ple_gate,
           w_ple_proj, norm_final):
    b, s, d = x_prompt.shape
    db, t_len, _ = x_sample.shape
    depth = w_in.shape[0]
    n_heads, page, hd = cache_k.shape[2:]
    n_pages = page_table.shape[1]
    n_experts = moe_router.shape[-1]
    ppb = BLOCK // page
    past_len = n_pages * page
    assert n_heads * hd == d and w_in.shape[-1] == N_IN_FIELDS * d and BLOCK % page == 0 and hd == LANE
    assert past_len % BLOCK == 0, "cached tail pages of the own block are not supported"
    assert s % ROW_TILE == 0 and s % page == 0 and t_len <= SUBLANE

    bs, ds = b * s, db * t_len
    tp = SUBLANE

    slopes = jnp.exp2(-8.0 * jnp.arange(1, n_heads + 1, dtype=F32) / n_heads)
    w_out_b = w_out.astype(BF16)
    dw1, dw3, dw2 = dense_w1.astype(BF16), dense_w3.astype(BF16), dense_w2.astype(BF16)
    mw1, mw3, mw2 = moe_w1.astype(BF16), moe_w3.astype(BF16), moe_w2.astype(BF16)
    wg_b, wp_b = w_ple_gate.astype(BF16), w_ple_proj.astype(BF16)
    n_exp_pad = 2 * SUBLANE
    assert n_experts <= n_exp_pad
    router_t = jnp.pad(moe_router.transpose(0, 2, 1), ((0, 0), (0, n_exp_pad - n_experts), (0, 0)))
    r_hi = router_t.astype(BF16)
    r_lo = (router_t - r_hi.astype(F32)).astype(BF16)
    router_cat = jnp.concatenate([r_hi, r_hi, r_lo], axis=-1)
    g_mix, g_ffn, g_ple = (g.reshape(depth, 1, d) for g in (norm_mix, norm_ffn, norm_ple))
    g_final = norm_final.reshape(1, d)

    x = x_prompt.reshape(bs, d)
    xs = x_sample.reshape(ds, d)
    p_dim = p_prompt.shape[-1]
    p_p = p_prompt.reshape(depth, bs, p_dim)
    p_s = p_sample.reshape(depth, ds, p_dim)

    means_t = _block_means(cache_k, page_table).transpose(0, 1, 3, 2, 4)
    state_t = state_conv.transpose(0, 2, 1, 3)
    batch_ix = jnp.arange(db)[:, None, None, None, None]
    page_off = jnp.arange(ppb)

    bm_moe = 3 * ROW_TILE
    bm_dense = _row_tile(bs, 4 * ROW_TILE)
    n_dense_chunks = bs // bm_dense
    dense_tables = (jnp.zeros((n_dense_chunks,), I32), jnp.arange(n_dense_chunks, dtype=I32),
                    jnp.full((n_dense_chunks,), bm_dense, I32))
    one_gate = jnp.ones((1, ds, 1), F32)

    k_all = jnp.zeros((depth, b, s // page, n_heads, page, hd), F32)
    v_all = jnp.zeros_like(k_all)
    cp, ksm, vsm, csm = [], [], [], []
    xn = _norm_rows(x, g_mix, 0)
    ys = None
    for i in range(depth):
        h = _inproj(xn, w_in, i)
        att_p, k_all, v_all = _moba_prompt(h, slopes, k_all, v_all, i, b, s, n_heads, hd)
        mix, u_tail = _mix_prompt(h, att_p, conv_w, i, b, s, d)
        if i % 2 == 0:
            x1, xf = _outproj(mix, x, w_out_b, g_ffn, i)
            x2 = _swiglu(xf, dw1, dw3, dw2, x1, *dense_tables, wsel=(i // 2,), bm=bm_dense, routed=False)
        else:
            x1, xf, e_t, gate_t = _outproj(mix, x, w_out_b, g_ffn, i, router_cat[i // 2], n_experts)
            dest, tok_sorted, gate_sorted, sub_active, chunk_e, chunk_blk, chunk_rows = _route(
                e_t[:MOE_TOPK].T, gate_t[:MOE_TOPK].T, n_experts, bm_moe)
            xg = _gather_rows(xf, tok_sorted, sub_active, d // LANE)
            yg = _swiglu(xg, mw1, mw3, mw2, gate_sorted, chunk_e, chunk_blk, chunk_rows, wsel=(i // 2,), bm=bm_moe,
                         routed=True)
            x2 = _combine(x1, yg, dest)
        last = i == depth - 1
        x, xn = _ple(x2, p_p, wg_b, wp_b, g_ple, i, g_final if last else g_mix[i + 1], F32 if last else BF16)

        hs = _s_inproj(xs, g_mix, w_in, i)
        qkv = hs[:, :3 * d].reshape(db, t_len, 3, n_heads, hd).transpose(2, 0, 3, 1, 4)
        qkv_p = jnp.pad(qkv, ((0, 0), (0, 0), (0, 0), (0, tp - t_len), (0, 0)))
        picks = _sample_select(qkv_p[0], means_t, i)[:, :, :t_len, :TOPK_BLOCKS]
        picks = picks.transpose(0, 2, 1, 3)
        cols = picks[..., None] * ppb + page_off
        phys = page_table[batch_ix, cols]
        att_s = _sample_attend(qkv_p[0], qkv_p[1], qkv_p[2], cache_k, cache_v, phys.reshape(-1).astype(I32),
                               picks.reshape(-1).astype(I32), slopes, i, t_len, past_len)
        att_s = att_s[:, :, :t_len].transpose(2, 0, 1, 3).reshape(t_len, db, d)
        fields_s = hs.reshape(db, t_len, N_IN_FIELDS, d).transpose(2, 1, 0, 3)
        mix_s, new_state = _mix_sample(fields_s, att_s, state_t[i], conv_w, i)
        mix_s = mix_s.transpose(1, 0, 2).reshape(ds, d)
        if i % 2 == 0:
            x1s, xns = _s_outproj(mix_s, xs, w_out, g_ffn, i)
            x2s = _s_swiglu(xns, dense_w1, dense_w3, dense_w2, one_gate, x1s, wsel=(i // 2,))
        else:
            x1s, xns, e_s, gate_s = _s_outproj(mix_s, xs, w_out, g_ffn, i, router_t[i // 2], n_experts)
            ex = jnp.arange(n_experts, dtype=I32)[:, None]
            gates = sum(jnp.where(e_s[k][None, :] == ex, gate_s[k][None, :], 0.0) for k in range(MOE_TOPK))
            x2s = _s_swiglu(xns, moe_w1, moe_w3, moe_w2, gates[:, :, None], x1s, wsel=(i // 2,))
        xs, ys = _s_ple(x2s, p_s, w_ple_gate, w_ple_proj, g_ple, i, g_final)

        cp.append(u_tail[:, SUBLANE - (CONV_W - 1):, :])
        ksm.append(qkv[1]); vsm.append(qkv[2])
        csm.append(new_state.transpose(1, 0, 2))

    y_prompt = xn.reshape(b, s, d)
    y_sample = ys.reshape(db, t_len, d)
    return (y_prompt, y_sample, k_all, v_all, jnp.stack(cp), jnp.stack(ksm), jnp.stack(vsm), jnp.stack(csm))
```

```python
import functools
import math

import jax
import jax.numpy as jnp
from jax import lax
from jax.experimental import pallas as pl
from jax.experimental.pallas import tpu as pltpu

F32 = jnp.float32
BF16 = jnp.bfloat16
I32 = jnp.int32

RMS_EPS = 1e-6
NEG = -1e30
BLOCK = 256
TOPK_BLOCKS = 3
MOE_TOPK = 2
CONV_W = 3
N_IN_FIELDS = 8
LOG2E = math.log2(math.e)

LANE = 128
SUBLANE = 8
ROW_TILE = 256
VMEM_LIMIT = 56 * 1024 * 1024
N_ALIBI_TERMS = 3


def _cparams(n_axes, vmem=VMEM_LIMIT):
    return pltpu.CompilerParams(dimension_semantics=("arbitrary",) * n_axes, vmem_limit_bytes=vmem)


def _row_tile(rows, max_tile):
    n = rows // ROW_TILE
    best = 1
    for k in range(1, n + 1):
        if n % k == 0 and k * ROW_TILE <= max_tile:
            best = k
    return best * ROW_TILE


def _col_tile(cols, max_tile):
    best = LANE
    for t in range(LANE, min(cols, max_tile) + 1, LANE):
        if cols % t == 0:
            best = t
    return best


def _rmsnorm(x, g):
    ms = jnp.mean(x * x, axis=-1, keepdims=True)
    return x * lax.rsqrt(ms + RMS_EPS) * g


def _dot_nt(a, b):
    return lax.dot_general(a, b, (((1,), (1,)), ((), ())), preferred_element_type=F32)


def _split_bf16(x, n=2):
    terms = []
    for _ in range(n - 1):
        t = x.astype(BF16)
        terms.append(t)
        x = x - t.astype(F32)
    terms.append(x.astype(BF16))
    return terms


def _dot3(x, w):
    return _dot3_split(x, *_split_bf16(w))


def _dot3_split(x, wh, wl):
    xh, xl = _split_bf16(x)
    dot = functools.partial(jnp.dot, preferred_element_type=F32)
    m = x.shape[0]
    if m % (2 * SUBLANE):
        return dot(xh, wh) + (dot(xl, wh) + dot(xh, wl))
    y = dot(jnp.concatenate([xh, xl], axis=0), wh)
    return y[:m] + (y[m:] + dot(xh, wl))


def _dot_nt3(a, b):
    (ah, al), (bh, bl) = _split_bf16(a), _split_bf16(b)
    m = a.shape[0]
    if m % (2 * SUBLANE):
        return _dot_nt(ah, bh) + (_dot_nt(al, bh) + _dot_nt(ah, bl))
    y = _dot_nt(jnp.concatenate([ah, al], axis=0), bh)
    return y[:m] + (y[m:] + _dot_nt(ah, bl))


def _block_of(pos):
    assert BLOCK & (BLOCK - 1) == 0
    return lax.shift_right_logical(pos, BLOCK.bit_length() - 1)


def _top_picks(cand, col, n, sentinel, axis=-1):
    picks = []
    colf = col.astype(F32)
    for _ in range(n):
        m = jnp.max(cand, axis=axis, keepdims=True)
        first = jnp.min(jnp.where(cand == m, colf, float(sentinel)), axis=axis, keepdims=True)
        picks.append(first.astype(I32))
        cand = jnp.where(colf == first, -jnp.inf, cand)
    return picks


def _load_row_slab(ref, lead, r0, n_rows, sl, n_sl):
    return ref[lead + (pl.ds(r0 * n_sl + sl, n_rows, stride=n_sl), slice(None))]


def _store_row_slabs(ref, r0, val):
    n_rows, d = val.shape
    n_sl = d // LANE
    for sl in range(n_sl):
        ref[pl.ds(r0 * n_sl + sl, n_rows, stride=n_sl), :] = val[:, sl * LANE:(sl + 1) * LANE]


def _norm_rows_kernel(x_ref, g_ref, o_ref):
    o_ref[...] = _rmsnorm(x_ref[...], g_ref[...]).astype(o_ref.dtype)


def _norm_rows(x, g, layer):
    rows, d = x.shape
    tm = _row_tile(rows, 768)
    return pl.pallas_call(
        _norm_rows_kernel,
        grid=(rows // tm,),
        in_specs=[pl.BlockSpec((tm, d), lambda i: (i, 0)), pl.BlockSpec((None, 1, d), lambda i: (layer, 0, 0))],
        out_specs=pl.BlockSpec((tm, d), lambda i: (i, 0)),
        out_shape=jax.ShapeDtypeStruct((rows, d), BF16),
        compiler_params=_cparams(1),
        name="norm_rows",
    )(x, g)


def _inproj_kernel(xn_ref, w_ref, o_ref):
    o_ref[...] = jnp.dot(xn_ref[...], w_ref[...].astype(BF16), preferred_element_type=F32)


def _inproj(xn, w, layer):
    rows, d = xn.shape
    n = w.shape[-1]
    tm = _row_tile(rows, 2816)
    tn = _col_tile(n, 512)
    return pl.pallas_call(
        _inproj_kernel,
        grid=(rows // tm, n // tn),
        in_specs=[
            pl.BlockSpec((tm, d), lambda i, j: (i, 0)),
            pl.BlockSpec((None, d, tn), lambda i, j: (layer, 0, j)),
        ],
        out_specs=pl.BlockSpec((tm, tn), lambda i, j: (i, j)),
        out_shape=jax.ShapeDtypeStruct((rows, n), F32),
        compiler_params=_cparams(2),
        name="inproj",
    )(xn, w)


def _moba_prompt_kernel(slopes_ref, q_ref, k_ref, v_ref, kall_ref, vall_ref, o_ref, ko_ref, vo_ref, kx_ref, vx_ref,
                        mt_ref, pen_ref, *, nb, hd, nbp, sel_rows):
    del kall_ref, vall_ref
    s_len = nb * BLOCK
    slope2 = slopes_ref[pl.program_id(1)] * LOG2E
    qscale = hd ** -0.5 * LOG2E
    k = k_ref[...]
    v = v_ref[...]
    ko_ref[...] = k.reshape(ko_ref.shape)
    vo_ref[...] = v.reshape(vo_ref.shape)

    lane = lax.broadcasted_iota(I32, (s_len, hd), 1)
    row = lax.broadcasted_iota(I32, (s_len, hd), 0)
    a_terms = _split_bf16(slope2 * row.astype(F32), N_ALIBI_TERMS)
    kext = jnp.where(lane == _block_of(row), 1.0, 0.0).astype(BF16)
    for t, a in enumerate(a_terms):
        kext = jnp.where(lane == nb + t, a, kext)
    kx_ref[...] = jnp.concatenate([k.astype(BF16), kext], axis=1)
    vx_ref[...] = jnp.concatenate([v.astype(BF16), jnp.where(lane == 0, 1.0, 0.0).astype(BF16)], axis=1)

    means = jnp.mean(k.reshape(nb, BLOCK, hd), axis=1)
    m_hi, m_lo = _split_bf16(means)
    mt_ref[...] = jnp.zeros_like(mt_ref)
    mt_ref[0:nb, :] = jnp.concatenate([m_hi, m_hi, m_lo], axis=1)

    col = lax.broadcasted_iota(I32, (sel_rows, nbp), 1)
    rsel = lax.broadcasted_iota(I32, (sel_rows, nbp), 0)
    ones_lane = (col >= nb) & (col < nb + N_ALIBI_TERMS)

    def select(c, carry):
        row0 = pl.multiple_of(c * sel_rows, sel_rows)
        q_hi, q_lo = _split_bf16(q_ref[pl.ds(row0, sel_rows), :])
        sc = _dot_nt(jnp.concatenate([q_hi, q_lo, q_hi], axis=1), mt_ref[...])
        n_past = _block_of(row0 + rsel)
        picks = _top_picks(jnp.where(col < n_past, sc, NEG), col, TOPK_BLOCKS, nbp)
        pen = jnp.where(col == n_past, 0.0, NEG)
        for r, p in enumerate(picks):
            pen = jnp.where((col == p) & (r < n_past), 0.0, pen)
        pen_ref[pl.ds(row0, sel_rows), :] = jnp.where(ones_lane, 1.0, pen).astype(BF16)
        return carry

    lax.fori_loop(0, s_len // sel_rows, select, 0)

    r_io = lax.broadcasted_iota(I32, (BLOCK, BLOCK), 0)
    c_io = lax.broadcasted_iota(I32, (BLOCK, BLOCK), 1)

    for i in range(nb):
        row0, win = i * BLOCK, i * BLOCK
        qx = jnp.concatenate([(q_ref[row0:row0 + BLOCK, :] * qscale).astype(BF16), pen_ref[row0:row0 + BLOCK, :]], axis=1)
        s_all = _dot_nt(qx, kx_ref[0:win + BLOCK, :])
        s_own = jnp.where(r_io >= c_io, s_all[:, win:], NEG)
        s_all = jnp.concatenate([s_all[:, :win], s_own], axis=1) if i else s_own
        m = jnp.max(s_all, axis=-1, keepdims=True)
        p = jnp.exp2(s_all - m).astype(BF16)
        o2 = jnp.dot(p, vx_ref[0:win + BLOCK, :], preferred_element_type=F32)
        o_ref[row0:row0 + BLOCK, :] = o2[:, :hd] / o2[:, hd:hd + 1]


def _moba_prompt(h, slopes, k_all, v_all, layer, b, s, n_heads, hd):
    assert s % BLOCK == 0 and TOPK_BLOCKS == 3
    nb = s // BLOCK
    nbp = hd
    assert nb + N_ALIBI_TERMS <= nbp
    sel_rows = min(s, 1024)
    assert s % sel_rows == 0
    n_pg, page = k_all.shape[2], k_all.shape[4]
    kern = functools.partial(_moba_prompt_kernel, nb=nb, hd=hd, nbp=nbp, sel_rows=sel_rows)
    pages = jax.ShapeDtypeStruct(k_all.shape, F32)
    page_spec = pl.BlockSpec((None, None, n_pg, None, page, hd), lambda bi, hi: (layer, bi, 0, hi, 0, 0))
    anyspec = pl.BlockSpec(memory_space=pl.ANY)
    return pl.pallas_call(
        kern,
        grid=(b, n_heads),
        in_specs=[
            pl.BlockSpec(memory_space=pltpu.SMEM),
            pl.BlockSpec((s, hd), lambda bi, hi: (bi, hi)),
            pl.BlockSpec((s, hd), lambda bi, hi: (bi, n_heads + hi)),
            pl.BlockSpec((s, hd), lambda bi, hi: (bi, 2 * n_heads + hi)),
            anyspec, anyspec,
        ],
        out_specs=[pl.BlockSpec((s, hd), lambda bi, hi: (bi, hi)), page_spec, page_spec],
        out_shape=[jax.ShapeDtypeStruct((b * s, n_heads * hd), F32), pages, pages],
        input_output_aliases={4: 1, 5: 2},
        scratch_shapes=[
            pltpu.VMEM((s, 2 * hd), BF16),
            pltpu.VMEM((s, 2 * hd), BF16),
            pltpu.VMEM((nbp, 3 * hd), BF16),
            pltpu.VMEM((s, nbp), BF16),
        ],
        compiler_params=_cparams(2),
        name="moba_prompt",
    )(slopes, h, h, h, k_all, v_all)


def _block_means_kernel(pt_ref, *refs, ppb, page, bps):
    del pt_ref
    o_ref = refs[ppb * bps]
    for blk in range(bps):
        tot = jnp.sum(refs[blk * ppb][0, 0], axis=1)
        for r in refs[blk * ppb + 1:(blk + 1) * ppb]:
            tot = tot + jnp.sum(r[0, 0], axis=1)
        o_ref[0, 0, blk] = tot * (1.0 / (ppb * page))


def _block_means(cache_k, page_table):
    depth, _, n_heads, page, hd = cache_k.shape
    db, n_pages = page_table.shape
    ppb = BLOCK // page
    nbk = n_pages // ppb
    bps = next(k for k in (4, 2, 1) if nbk % k == 0)
    pt = page_table.reshape(-1)

    def page_map(jp, l, b, n, pt_ref):
        return (l, pt_ref[b * n_pages + n * (ppb * bps) + jp], 0, 0, 0)

    return pl.pallas_call(
        functools.partial(_block_means_kernel, ppb=ppb, page=page, bps=bps),
        grid_spec=pltpu.PrefetchScalarGridSpec(
            num_scalar_prefetch=1,
            grid=(depth, db, nbk // bps),
            in_specs=[pl.BlockSpec((1, 1, n_heads, page, hd), functools.partial(page_map, jp))
                      for jp in range(ppb * bps)],
            out_specs=pl.BlockSpec((1, 1, bps, n_heads, hd), lambda l, b, n, pt_ref: (l, b, n, 0, 0)),
        ),
        out_shape=jax.ShapeDtypeStruct((depth, db, nbk, n_heads, hd), F32),
        compiler_params=_cparams(3),
        name="block_means",
    )(pt, *([cache_k] * (ppb * bps)))


def _sample_select_kernel(q_ref, m_ref, o_ref, *, nbk):
    q = q_ref[0]
    means = m_ref[0, 0]
    sc = lax.dot_general(q, means, (((2,), (2,)), ((0,), (0,))),
                         precision=lax.Precision.HIGHEST, preferred_element_type=F32)
    col = lax.broadcasted_iota(I32, sc.shape, 2)
    picks = _top_picks(sc, col, TOPK_BLOCKS, nbk)
    lane = lax.broadcasted_iota(I32, o_ref.shape[1:], 2)
    o_ref[0] = jnp.where(lane == 0, picks[0], jnp.where(lane == 1, picks[1], picks[2]))


def _sample_select(q_s, means_t, layer):
    db, n_heads, tp, hd = q_s.shape
    nbk = means_t.shape[3]
    assert nbk >= TOPK_BLOCKS
    return pl.pallas_call(
        functools.partial(_sample_select_kernel, nbk=nbk),
        grid=(db,),
        in_specs=[
            pl.BlockSpec((1, n_heads, tp, hd), lambda b: (b, 0, 0, 0)),
            pl.BlockSpec((1, 1, n_heads, nbk, hd), lambda b: (layer, b, 0, 0, 0)),
        ],
        out_specs=pl.BlockSpec((1, n_heads, tp, LANE), lambda b: (b, 0, 0, 0)),
        out_shape=jax.ShapeDtypeStruct((db, n_heads, tp, LANE), I32),
        compiler_params=_cparams(1),
        name="sample_select",
    )(q_s, means_t)


def _sample_attend_kernel(phys_ref, blk_ref, slopes_ref, q_ref, kn_ref, vn_ref, ck_hbm, cv_hbm, o_ref, kbuf, vbuf, sems,
                          *, layer, t_len, n_sel, past_len, hd, page):
    b, hh = pl.program_id(0), pl.program_id(1)
    n_heads = pl.num_programs(1)
    step = b * n_heads + hh
    n_steps = pl.num_programs(0) * n_heads

    def page_copies(st, slot):
        sb, sh = st // n_heads, st % n_heads
        cps = []
        for t in range(t_len):
            for c in range(n_sel):
                pg = phys_ref[((sb * t_len + t) * n_heads + sh) * n_sel + c]
                j = t * n_sel + c
                cps.append(pltpu.make_async_copy(ck_hbm.at[layer, pg, sh], kbuf.at[slot, j], sems.at[0, slot]))
                cps.append(pltpu.make_async_copy(cv_hbm.at[layer, pg, sh], vbuf.at[slot, j], sems.at[1, slot]))
        return cps

    @pl.when(step == 0)
    def _():
        for cp in page_copies(step, 0):
            cp.start()

    @pl.when(step + 1 < n_steps)
    def _():
        for cp in page_copies(step + 1, (step + 1) % 2):
            cp.start()

    slot = step % 2
    for cp in page_copies(step, slot):
        cp.wait()

    slope = slopes_ref[hh]
    scale = hd ** -0.5
    ppb = BLOCK // page
    qf = q_ref[0, 0] * scale
    tp = qf.shape[0]
    s_own_all = _dot_nt3(qf, kn_ref[0, 0])
    vn = vn_ref[0, 0]
    lane = lax.broadcasted_iota(I32, (1, page), 1)
    tcol = lax.broadcasted_iota(I32, (1, tp), 1)
    outs = []
    for t in range(t_len):
        tq = past_len + t
        pos = jnp.concatenate(
            [blk_ref[((b * t_len + t) * n_heads + hh) * TOPK_BLOCKS + c // ppb] * BLOCK + (c % ppb) * page + lane
             for c in range(n_sel)], axis=1)
        k_sel = kbuf[slot, t * n_sel:(t + 1) * n_sel].reshape(n_sel * page, hd)
        v_sel = vbuf[slot, t * n_sel:(t + 1) * n_sel].reshape(n_sel * page, hd)
        s_sel = _dot_nt3(qf, k_sel)[t:t + 1, :] - slope * (tq - pos).astype(F32)
        s_own = s_own_all[t:t + 1, :] - slope * (t - tcol).astype(F32)
        s_own = jnp.where(tcol <= t, s_own, NEG)
        m = jnp.maximum(jnp.max(s_own, axis=-1, keepdims=True), jnp.max(s_sel, axis=-1, keepdims=True))
        p_own = jnp.exp(s_own - m)
        p_sel = jnp.exp(s_sel - m)
        l = jnp.sum(p_own, axis=-1, keepdims=True) + jnp.sum(p_sel, axis=-1, keepdims=True)
        acc = _dot3(jnp.broadcast_to(p_own, (tp, tp)), vn) + _dot3(jnp.broadcast_to(p_sel, (tp, n_sel * page)), v_sel)
        outs.append(acc[0:1, :] / l)
    outs += [jnp.zeros((1, hd), F32)] * (tp - t_len)
    o_ref[0, 0] = jnp.concatenate(outs, axis=0)


def _sample_attend(q_s, k_new, v_new, cache_k, cache_v, phys, blk, slopes, layer, t_len, past_len):
    db, n_heads, tp, hd = q_s.shape
    page = cache_k.shape[3]
    n_sel = TOPK_BLOCKS * (BLOCK // page)
    n_pg = t_len * n_sel
    small = pl.BlockSpec((1, 1, tp, hd), lambda b, hh, phys_ref, blk_ref: (b, hh, 0, 0))
    anyspec = pl.BlockSpec(memory_space=pl.ANY)
    kern = functools.partial(_sample_attend_kernel, layer=layer, t_len=t_len, n_sel=n_sel, past_len=past_len, hd=hd,
                             page=page)
    return pl.pallas_call(
        kern,
        grid_spec=pltpu.PrefetchScalarGridSpec(
            num_scalar_prefetch=2,
            grid=(db, n_heads),
            in_specs=[pl.BlockSpec(memory_space=pltpu.SMEM), small, small, small, anyspec, anyspec],
            out_specs=small,
            scratch_shapes=[pltpu.VMEM((2, n_pg, page, hd), F32), pltpu.VMEM((2, n_pg, page, hd), F32),
                            pltpu.SemaphoreType.DMA((2, 2))],
        ),
        out_shape=jax.ShapeDtypeStruct((db, n_heads, tp, hd), F32),
        compiler_params=_cparams(2),
        name="sample_attend",
    )(phys, blk, slopes, q_s, k_new, v_new, cache_k, cache_v)


def _mix(gb, ga, gcv, att, u0, u1, u2, cw):
    conv = cw[0:1] * u0 + cw[1:2] * u1 + cw[2:3] * u2
    return jax.nn.sigmoid(ga) * att + jax.nn.sigmoid(gcv) * (gb * conv)


def _mix_prompt_kernel(gb_ref, gc_ref, xc_ref, ga_ref, gcv_ref, gch_ref, xch_ref, att_ref, cw_ref, o_ref, us_ref,
                       *, tiles_per_seq, n_tiles):
    i = pl.program_id(0)

    @pl.when(i >= n_tiles)
    def _():
        o_ref[...] = jnp.zeros_like(o_ref)

    @pl.when(i < n_tiles)
    def _():
        cw = cw_ref[...]
        u = gc_ref[...] * xc_ref[...]
        halo = jnp.where(i % tiles_per_seq == 0, 0.0, gch_ref[...] * xch_ref[...])
        u1 = pltpu.roll(u, 1, 0)
        u2 = pltpu.roll(u, 2, 0)
        o_ref[...] = _mix(gb_ref[...], ga_ref[...], gcv_ref[...], att_ref[...], u2, u1, u, cw).astype(o_ref.dtype)
        row = lax.broadcasted_iota(I32, (SUBLANE, u.shape[1]), 0)
        f1 = jnp.where(row < 1, pltpu.roll(halo, 1, 0), u1[0:SUBLANE])
        f2 = jnp.where(row < 2, pltpu.roll(halo, 2, 0), u2[0:SUBLANE])
        o_ref[0:SUBLANE, :] = _mix(gb_ref[0:SUBLANE, :], ga_ref[0:SUBLANE, :], gcv_ref[0:SUBLANE, :],
                                   att_ref[0:SUBLANE, :], f2, f1, u[0:SUBLANE], cw).astype(o_ref.dtype)
        us_ref[...] = u[u.shape[0] - SUBLANE:, :]


def _mix_prompt(h, att, conv_w, layer, b, s, d):
    assert CONV_W == 3
    tm = ROW_TILE
    tps = s // tm
    hpt = tm // SUBLANE
    n_tiles = b * tps
    rows = h.shape[0]
    clamp = lambda i: jnp.minimum(i, n_tiles - 1)

    def field(f):
        return pl.BlockSpec((tm, d), lambda i: (clamp(i), f))

    def halo(f):
        return pl.BlockSpec((SUBLANE, d), lambda i: (jnp.maximum(clamp(i) * hpt - 1, 0), f))

    return pl.pallas_call(
        functools.partial(_mix_prompt_kernel, tiles_per_seq=tps, n_tiles=n_tiles),
        grid=(rows // tm,),
        in_specs=[field(3), field(4), field(5), field(6), field(7), halo(4), halo(5),
                  pl.BlockSpec((tm, d), lambda i: (clamp(i), 0)),
                  pl.BlockSpec((None, CONV_W, d), lambda i: (layer, 0, 0))],
        out_specs=[pl.BlockSpec((tm, d), lambda i: (i, 0)),
                   pl.BlockSpec((None, SUBLANE, d), lambda i: (clamp(i) // tps, 0, 0))],
        out_shape=[jax.ShapeDtypeStruct((rows, d), BF16), jax.ShapeDtypeStruct((b, SUBLANE, d), F32)],
        compiler_params=_cparams(1),
        name="mix_prompt",
    )(h, h, h, h, h, h, h, att, conv_w)


def _mix_sample_kernel(f_ref, att_ref, st_ref, cw_ref, o_ref, ns_ref, *, t_len):
    cw = cw_ref[...]
    u = [st_ref[0], st_ref[1]] + [f_ref[4, t] * f_ref[5, t] for t in range(t_len)]
    for t in range(t_len):
        o_ref[t] = _mix(f_ref[3, t], f_ref[6, t], f_ref[7, t], att_ref[t], u[t], u[t + 1], u[t + 2], cw).astype(o_ref.dtype)
    ns_ref[0] = u[t_len]
    ns_ref[1] = u[t_len + 1]


def _mix_sample(fields, att, state, conv_w, layer):
    _, t_len, db, d = fields.shape
    return pl.pallas_call(
        functools.partial(_mix_sample_kernel, t_len=t_len),
        grid=(1,),
        in_specs=[pl.BlockSpec(fields.shape, lambda i: (0, 0, 0, 0)),
                  pl.BlockSpec(att.shape, lambda i: (0, 0, 0)),
                  pl.BlockSpec(state.shape, lambda i: (0, 0, 0)),
                  pl.BlockSpec((None, CONV_W, d), lambda i: (layer, 0, 0))],
        out_specs=[pl.BlockSpec((t_len, db, d), lambda i: (0, 0, 0)),
                   pl.BlockSpec((CONV_W - 1, db, d), lambda i: (0, 0, 0))],
        out_shape=[jax.ShapeDtypeStruct((t_len, db, d), F32), jax.ShapeDtypeStruct((CONV_W - 1, db, d), F32)],
        compiler_params=_cparams(1),
        name="mix_sample",
    )(fields, att, state, conv_w)


def _outproj_kernel(*refs, n_experts):
    if n_experts:
        mix_ref, x_ref, w_ref, g_ref, r_ref, x1_ref, xnf_ref, e_ref, gt_ref = refs
    else:
        mix_ref, x_ref, w_ref, g_ref, x1_ref, xn_ref = refs
    x1 = x_ref[...] + jnp.dot(mix_ref[...], w_ref[...], preferred_element_type=F32)
    x1_ref[...] = x1
    xn = _rmsnorm(x1, g_ref[...])
    if not n_experts:
        xn_ref[...] = xn.astype(BF16)
        return
    _store_row_slabs(xnf_ref, 0, xn)
    x_hi, x_lo = _split_bf16(xn)
    logits = _dot_nt(r_ref[...], jnp.concatenate([x_hi, x_lo, x_hi], axis=1))
    ex = lax.broadcasted_iota(I32, logits.shape, 0)
    logits = jnp.where(ex < n_experts, logits, -jnp.inf)
    (i1, i2) = _top_picks(logits, ex, MOE_TOPK, logits.shape[0], axis=0)
    m1 = jnp.max(logits, axis=0, keepdims=True)
    m2 = jnp.max(jnp.where(ex == i1, -jnp.inf, logits), axis=0, keepdims=True)
    e2 = jnp.exp(m2 - m1)
    den = 1.0 + e2
    orow = lax.broadcasted_iota(I32, e_ref.shape, 0)
    e_ref[...] = jnp.where(orow == 0, i1, i2)
    gt_ref[...] = jnp.where(orow == 0, 1.0 / den, e2 / den)


def _outproj(mix, x, w, g, layer, router_t=None, n_experts=0):
    rows, d = x.shape
    tm = _row_tile(rows, 256)
    rowspec = pl.BlockSpec((tm, d), lambda i: (i, 0))
    in_specs = [rowspec, rowspec,
                pl.BlockSpec((None, d, d), lambda i: (layer, 0, 0), pipeline_mode=pl.Buffered(1)),
                pl.BlockSpec((None, 1, d), lambda i: (layer, 0, 0))]
    args = [mix, x, w, g]
    if n_experts:
        tspec = pl.BlockSpec((SUBLANE, tm), lambda i: (0, i))
        in_specs.append(pl.BlockSpec(router_t.shape, lambda i: (0, 0)))
        out_specs = [rowspec, pl.BlockSpec((tm * (d // LANE), LANE), lambda i: (i, 0)), tspec, tspec]
        out_shape = [jax.ShapeDtypeStruct((rows, d), F32), jax.ShapeDtypeStruct((rows * (d // LANE), LANE), F32),
                     jax.ShapeDtypeStruct((SUBLANE, rows), I32), jax.ShapeDtypeStruct((SUBLANE, rows), F32)]
        args.append(router_t)
    else:
        out_specs = [rowspec, rowspec]
        out_shape = [jax.ShapeDtypeStruct((rows, d), F32), jax.ShapeDtypeStruct((rows, d), BF16)]
    return pl.pallas_call(
        functools.partial(_outproj_kernel, n_experts=n_experts),
        grid=(rows // tm,),
        in_specs=in_specs, out_specs=out_specs, out_shape=out_shape,
        compiler_params=_cparams(1),
        name="outproj",
    )(*args)


def _swiglu_kernel(ce_ref, cb_ref, cr_ref, x_ref, w1_ref, w3_ref, w2_ref, s_ref, o_ref, acc_ref, *xb_scratch,
                   n_f, bm, routed):
    del ce_ref, cb_ref
    c, f = pl.program_id(0), pl.program_id(1)
    rows = cr_ref[c]
    nsub = (rows + (ROW_TILE - 1)) // ROW_TILE
    n_sl = acc_ref.shape[1] // LANE

    if routed:
        xb_ref, = xb_scratch

        @pl.when(f == 0)
        def _():
            def conv(sb, carry):
                r0 = pl.multiple_of(sb * ROW_TILE, ROW_TILE)
                for sl in range(n_sl):
                    xb_ref[pl.ds(r0, ROW_TILE), sl * LANE:(sl + 1) * LANE] = _load_row_slab(
                        x_ref, (), r0, ROW_TILE, sl, n_sl).astype(BF16)
                return carry

            lax.fori_loop(0, nsub, conv, 0)
    else:
        xb_ref = x_ref

    def accumulate(r0, n_rows):
        xs = xb_ref[pl.ds(r0, n_rows), :]
        h1 = jnp.dot(xs, w1_ref[...], preferred_element_type=F32)
        h3 = jnp.dot(xs, w3_ref[...], preferred_element_type=F32)
        hh = (jax.nn.silu(h1) * h3).astype(BF16)
        part = jnp.dot(hh, w2_ref[...], preferred_element_type=F32)

        @pl.when(f == 0)
        def _():
            acc_ref[pl.ds(r0, n_rows), :] = part

        @pl.when(f > 0)
        def _():
            acc_ref[pl.ds(r0, n_rows), :] += part

    @pl.when(nsub == bm // ROW_TILE)
    def _():
        accumulate(0, bm)

    @pl.when(nsub < bm // ROW_TILE)
    def _():
        def sub(sb, carry):
            accumulate(pl.multiple_of(sb * ROW_TILE, ROW_TILE), ROW_TILE)
            return carry

        lax.fori_loop(0, nsub, sub, 0)

    @pl.when(f == n_f - 1)
    def _():
        def fin(sb, carry):
            r0 = pl.multiple_of(sb * ROW_TILE, ROW_TILE)
            a = acc_ref[pl.ds(r0, ROW_TILE), :]
            if routed:
                _store_row_slabs(o_ref, r0, a * s_ref[pl.ds(r0, ROW_TILE), :])
            else:
                o_ref[pl.ds(r0, ROW_TILE), :] = s_ref[pl.ds(r0, ROW_TILE), :] + a
            return carry

        lax.fori_loop(0, nsub, fin, 0)

        def blank(sb, carry):
            rep = o_ref.shape[0] // bm
            r0 = pl.multiple_of(sb * ROW_TILE * rep, ROW_TILE * rep)
            o_ref[pl.ds(r0, ROW_TILE * rep), :] = jnp.zeros((ROW_TILE * rep, o_ref.shape[1]), F32)
            return carry

        lax.fori_loop(nsub, bm // ROW_TILE, blank, 0)


def _swiglu(x, w1, w3, w2, side, chunk_e, chunk_blk, chunk_rows, wsel, bm, routed):
    d = w1.shape[-2]
    ff = w1.shape[-1]
    tf = _col_tile(ff, 1024 if routed else 512)
    n_f = ff // tf
    n_chunks = chunk_e.shape[0]
    rep = d // LANE if routed else 1
    lead = len(wsel)
    per_expert = w1.ndim == lead + 3

    def fidx(c, f, cr):
        return jnp.where(cr[c] > 0, f, n_f - 1)

    def w13_map(c, f, ce, cb, cr):
        return wsel + ((ce[c],) if per_expert else ()) + (0, fidx(c, f, cr))

    def w2_map(c, f, ce, cb, cr):
        return wsel + ((ce[c],) if per_expert else ()) + (fidx(c, f, cr), 0)

    nlead = lead + (1 if per_expert else 0)
    w13_spec = pl.BlockSpec((None,) * nlead + (d, tf), w13_map)
    w2_spec = pl.BlockSpec((None,) * nlead + (tf, d), w2_map)
    once = dict(pipeline_mode=pl.Buffered(1))
    x_spec = pl.BlockSpec((bm * rep, x.shape[1]), lambda c, f, ce, cb, cr: (cb[c], 0), **(once if routed else {}))
    o_spec = pl.BlockSpec((bm * rep, x.shape[1]), lambda c, f, ce, cb, cr: (c, 0), **once)
    side_spec = pl.BlockSpec((bm, side.shape[1]), lambda c, f, ce, cb, cr: (cb[c], 0), **({} if routed else once))
    scratch = [pltpu.VMEM((bm, d), F32)] + ([pltpu.VMEM((bm, d), BF16)] if routed else [])
    return pl.pallas_call(
        functools.partial(_swiglu_kernel, n_f=n_f, bm=bm, routed=routed),
        grid_spec=pltpu.PrefetchScalarGridSpec(
            num_scalar_prefetch=3,
            grid=(n_chunks, n_f),
            in_specs=[x_spec, w13_spec, w13_spec, w2_spec, side_spec],
            out_specs=o_spec,
            scratch_shapes=scratch,
        ),
        out_shape=jax.ShapeDtypeStruct(x.shape, F32),
        compiler_params=_cparams(2),
        name="swiglu",
    )(chunk_e, chunk_blk, chunk_rows, x, w1, w3, w2, side)


def _gather_rows_kernel(tok_ref, act_ref, x_hbm, o_ref, sem, *, n_sl):
    sb = pl.program_id(0)
    base = sb * ROW_TILE

    def row(ref, r):
        return ref.at[pl.ds(pl.multiple_of(r * n_sl, n_sl), n_sl)]

    @pl.when(act_ref[sb] > 0)
    def _():
        def start(r, carry):
            pltpu.make_async_copy(row(x_hbm, tok_ref[base + r]), row(o_ref, r), sem).start()
            return carry

        lax.fori_loop(0, ROW_TILE, start, 0)

        def wait(r, carry):
            pltpu.make_async_copy(row(x_hbm, 0), row(o_ref, r), sem).wait()
            return carry

        lax.fori_loop(0, ROW_TILE, wait, 0)

    @pl.when(act_ref[sb] == 0)
    def _():
        o_ref[...] = jnp.zeros_like(o_ref)


def _gather_rows(x, tok_sorted, sub_active, n_sl):
    n_out = tok_sorted.shape[0]
    return pl.pallas_call(
        functools.partial(_gather_rows_kernel, n_sl=n_sl),
        grid_spec=pltpu.PrefetchScalarGridSpec(
            num_scalar_prefetch=2,
            grid=(n_out // ROW_TILE,),
            in_specs=[pl.BlockSpec(memory_space=pl.ANY)],
            out_specs=pl.BlockSpec((ROW_TILE * n_sl, x.shape[1]), lambda sb, tok_ref, act_ref: (sb, 0)),
            scratch_shapes=[pltpu.SemaphoreType.DMA(())],
        ),
        out_shape=jax.ShapeDtypeStruct((n_out * n_sl, x.shape[1]), x.dtype),
        compiler_params=_cparams(1),
        name="gather_rows",
    )(tok_sorted, sub_active, x)


def _combine_kernel(dest_ref, x_ref, y_hbm, o_ref, buf_ref, sem):
    i = pl.program_id(0)
    tm, d = x_ref.shape
    n_sl = d // LANE
    base = i * tm * MOE_TOPK

    def row(ref, r):
        return ref.at[pl.ds(pl.multiple_of(r * n_sl, n_sl), n_sl)]

    def start(r, carry):
        for k in range(MOE_TOPK):
            pltpu.make_async_copy(row(y_hbm, dest_ref[base + r * MOE_TOPK + k]), row(buf_ref.at[k], r), sem).start()
        return carry

    lax.fori_loop(0, tm, start, 0)

    def wait(r, carry):
        for k in range(MOE_TOPK):
            pltpu.make_async_copy(row(y_hbm, 0), row(buf_ref.at[k], r), sem).wait()
        return carry

    lax.fori_loop(0, tm, wait, 0)
    for sl in range(n_sl):
        f = _load_row_slab(buf_ref, (0,), 0, tm, sl, n_sl)
        for k in range(1, MOE_TOPK):
            f = f + _load_row_slab(buf_ref, (k,), 0, tm, sl, n_sl)
        o_ref[:, sl * LANE:(sl + 1) * LANE] = x_ref[:, sl * LANE:(sl + 1) * LANE] + f


def _combine(x, y, dest):
    rows, d = x.shape
    tm = ROW_TILE
    rowspec = pl.BlockSpec((tm, d), lambda i, dest_ref: (i, 0))
    return pl.pallas_call(
        _combine_kernel,
        grid_spec=pltpu.PrefetchScalarGridSpec(
            num_scalar_prefetch=1,
            grid=(rows // tm,),
            in_specs=[rowspec, pl.BlockSpec(memory_space=pl.ANY)],
            out_specs=rowspec,
            scratch_shapes=[pltpu.VMEM((MOE_TOPK, tm * (d // LANE), y.shape[1]), F32), pltpu.SemaphoreType.DMA(())],
        ),
        out_shape=jax.ShapeDtypeStruct(x.shape, F32),
        compiler_params=_cparams(1),
        name="combine",
    )(dest, x, y)


def _route(e_idx, gates, n_experts, bm):
    rows = e_idx.shape[0]
    ns = rows * MOE_TOPK
    n_chunks = ns // bm + n_experts
    flat_e = e_idx.reshape(-1)
    onehot = (flat_e[:, None] == jnp.arange(n_experts, dtype=I32)[None, :]).astype(I32)
    csum = jnp.cumsum(onehot, axis=0)
    rank = jnp.take_along_axis(csum, flat_e[:, None], axis=1)[:, 0] - 1
    counts = csum[-1]
    padded = (counts + bm - 1) // bm * bm
    end_pad = jnp.cumsum(padded)
    start_pad = end_pad - padded
    dest = (start_pad[flat_e] + rank).astype(I32)
    n_used = end_pad[-1] // bm
    cidx = jnp.minimum(jnp.arange(n_chunks, dtype=I32), n_used - 1)
    chunk_e = jnp.minimum(jnp.sum((end_pad[None, :] <= (cidx * bm)[:, None]).astype(I32), axis=1), n_experts - 1)
    rows_left = counts[chunk_e] - (cidx * bm - start_pad[chunk_e])
    chunk_rows = jnp.where(jnp.arange(n_chunks) < n_used, jnp.clip(rows_left, 0, bm), 0).astype(I32)
    n_buf = n_chunks * bm
    tok_sorted = jnp.zeros((n_buf,), I32).at[dest].set(jnp.arange(ns, dtype=I32) // MOE_TOPK)
    gate_sorted = jnp.zeros((n_buf,), F32).at[dest].set(gates.reshape(-1))
    sub_per = bm // ROW_TILE
    sub_in_chunk = jnp.arange(n_chunks * sub_per, dtype=I32) % sub_per
    sub_active = (sub_in_chunk * ROW_TILE < jnp.repeat(chunk_rows, sub_per)).astype(I32)
    return dest, tok_sorted, gate_sorted[:, None], sub_active, chunk_e.astype(I32), cidx.astype(I32), chunk_rows


def _ple_kernel(x_ref, p_ref, wg_ref, wp_ref, g_ref, gn_ref, o_ref, n_ref):
    x = x_ref[...]
    z = jnp.dot(_rmsnorm(x, g_ref[...]).astype(BF16), wg_ref[...], preferred_element_type=F32)
    pp = jnp.dot(p_ref[...].astype(BF16), wp_ref[...], preferred_element_type=F32)
    x3 = x + jax.nn.sigmoid(z) * pp
    o_ref[...] = x3
    n_ref[...] = _rmsnorm(x3, gn_ref[...]).astype(n_ref.dtype)


def _ple(x, p, wg, wp, g, layer, g_next, next_dtype):
    rows, d = x.shape
    pd = p.shape[-1]
    tm = _row_tile(rows, 256)
    rowspec = pl.BlockSpec((tm, d), lambda i: (i, 0))
    return pl.pallas_call(
        _ple_kernel,
        grid=(rows // tm,),
        in_specs=[rowspec,
                  pl.BlockSpec((None, tm, pd), lambda i: (layer, i, 0)),
                  pl.BlockSpec((None, d, d), lambda i: (layer, 0, 0), pipeline_mode=pl.Buffered(1)),
                  pl.BlockSpec((None, pd, d), lambda i: (layer, 0, 0), pipeline_mode=pl.Buffered(1)),
                  pl.BlockSpec((None, 1, d), lambda i: (layer, 0, 0)),
                  pl.BlockSpec((1, d), lambda i: (0, 0))],
        out_specs=[rowspec, rowspec],
        out_shape=[jax.ShapeDtypeStruct((rows, d), F32), jax.ShapeDtypeStruct((rows, d), next_dtype)],
        compiler_params=_cparams(1),
        name="ple",
    )(x, p, wg, wp, g, g_next)


def _s_inproj_kernel(x_ref, g_ref, w_ref, o_ref):
    o_ref[...] = _dot3(_rmsnorm(x_ref[...], g_ref[...]), w_ref[...])


def _s_inproj(x, g, w, layer):
    n_rows, d = x.shape
    n = w.shape[-1]
    tn = _col_tile(n, 1024)
    return pl.pallas_call(
        _s_inproj_kernel,
        grid=(n // tn,),
        in_specs=[pl.BlockSpec((n_rows, d), lambda j: (0, 0)),
                  pl.BlockSpec((None, 1, d), lambda j: (layer, 0, 0)),
                  pl.BlockSpec((None, d, tn), lambda j: (layer, 0, j))],
        out_specs=pl.BlockSpec((n_rows, tn), lambda j: (0, j)),
        out_shape=jax.ShapeDtypeStruct((n_rows, n), F32),
        compiler_params=_cparams(1),
        name="s_inproj",
    )(x, g, w)


def _s_outproj_kernel(*refs, n_experts):
    if n_experts:
        mix_ref, x_ref, w_ref, g_ref, r_ref, x1_ref, xn_ref, e_ref, gt_ref = refs
    else:
        mix_ref, x_ref, w_ref, g_ref, x1_ref, xn_ref = refs
    x1 = x_ref[...] + _dot3(mix_ref[...], w_ref[...])
    x1_ref[...] = x1
    xn = _rmsnorm(x1, g_ref[...])
    xn_ref[...] = xn
    if n_experts:
        logits = _dot_nt3(r_ref[...], xn)
        ex = lax.broadcasted_iota(I32, logits.shape, 0)
        logits = jnp.where(ex < n_experts, logits, -jnp.inf)
        (i1, i2) = _top_picks(logits, ex, MOE_TOPK, logits.shape[0], axis=0)
        m1 = jnp.max(logits, axis=0, keepdims=True)
        m2 = jnp.max(jnp.where(ex == i1, -jnp.inf, logits), axis=0, keepdims=True)
        e2 = jnp.exp(m2 - m1)
        den = 1.0 + e2
        orow = lax.broadcasted_iota(I32, e_ref.shape, 0)
        e_ref[...] = jnp.where(orow == 0, i1, i2)
        gt_ref[...] = jnp.where(orow == 0, 1.0 / den, e2 / den)


def _s_outproj(mix, x, w, g, layer, router_t=None, n_experts=0):
    n_rows, d = x.shape
    full = pl.BlockSpec((n_rows, d), lambda i: (0, 0))
    in_specs = [full, full,
                pl.BlockSpec((None, d, d), lambda i: (layer, 0, 0), pipeline_mode=pl.Buffered(1)),
                pl.BlockSpec((None, 1, d), lambda i: (layer, 0, 0))]
    out_specs = [full, full]
    out_shape = [jax.ShapeDtypeStruct((n_rows, d), F32), jax.ShapeDtypeStruct((n_rows, d), F32)]
    args = [mix, x, w, g]
    if n_experts:
        tspec = pl.BlockSpec((SUBLANE, n_rows), lambda i: (0, 0))
        in_specs.append(pl.BlockSpec(router_t.shape, lambda i: (0, 0)))
        out_specs += [tspec, tspec]
        out_shape += [jax.ShapeDtypeStruct((SUBLANE, n_rows), I32), jax.ShapeDtypeStruct((SUBLANE, n_rows), F32)]
        args.append(router_t)
    return pl.pallas_call(
        functools.partial(_s_outproj_kernel, n_experts=n_experts),
        grid=(1,),
        in_specs=in_specs, out_specs=out_specs, out_shape=out_shape,
        compiler_params=_cparams(1),
        name="s_outproj",
    )(*args)


def _s_swiglu_kernel(x_ref, w1_ref, w3_ref, w2_ref, gate_ref, res_ref, o_ref, w1b_ref, w3b_ref, w2b_ref, acc_ref):
    e, f = pl.program_id(0), pl.program_id(1)
    (w1h, w1l), (w3h, w3l), (w2h, w2l) = (_split_bf16(r[...]) for r in (w1_ref, w3_ref, w2_ref))
    w1b_ref[...], w3b_ref[...], w2b_ref[...] = w1h, w3h, w2h

    @pl.when((e == 0) & (f == 0))
    def _():
        acc_ref[...] = jnp.zeros_like(acc_ref)

    xs = x_ref[...]
    hh = jax.nn.silu(_dot3_split(xs, w1h, w1l)) * _dot3_split(xs, w3h, w3l)
    gate = gate_ref[...]
    acc_ref[...] += jnp.where(gate != 0.0, gate * _dot3_split(hh, w2h, w2l), 0.0)

    @pl.when((e == pl.num_programs(0) - 1) & (f == pl.num_programs(1) - 1))
    def _():
        o_ref[...] = res_ref[...] + acc_ref[...]


def _s_swiglu(x, w1, w3, w2, gates, res, wsel):
    n_rows, d = x.shape
    ff = w1.shape[-1]
    tf = _col_tile(ff, 512)
    n_e = gates.shape[0]
    per_expert = w1.ndim == len(wsel) + 3
    lead = (None,) * (len(wsel) + (1 if per_expert else 0))
    esel = (lambda e: (e,)) if per_expert else (lambda e: ())
    full = pl.BlockSpec((n_rows, d), lambda e, f: (0, 0))
    w13_spec = pl.BlockSpec(lead + (d, tf), lambda e, f: wsel + esel(e) + (0, f))
    w2_spec = pl.BlockSpec(lead + (tf, d), lambda e, f: wsel + esel(e) + (f, 0))
    elead = lead[len(wsel):]
    w13b_spec = pl.BlockSpec(elead + (d, tf), lambda e, f: esel(e) + (0, f))
    w2b_spec = pl.BlockSpec(elead + (tf, d), lambda e, f: esel(e) + (f, 0))
    return pl.pallas_call(
        _s_swiglu_kernel,
        grid=(n_e, ff // tf),
        in_specs=[full, w13_spec, w13_spec, w2_spec, pl.BlockSpec((None, n_rows, 1), lambda e, f: (e, 0, 0)), full],
        out_specs=[full, w13b_spec, w13b_spec, w2b_spec],
        out_shape=[jax.ShapeDtypeStruct((n_rows, d), F32), jax.ShapeDtypeStruct(w1.shape[len(wsel):], BF16),
                   jax.ShapeDtypeStruct(w3.shape[len(wsel):], BF16), jax.ShapeDtypeStruct(w2.shape[len(wsel):], BF16)],
        scratch_shapes=[pltpu.VMEM((n_rows, d), F32)],
        compiler_params=_cparams(2),
        name="s_swiglu",
    )(x, w1, w3, w2, gates, res)


def _s_ple_kernel(x_ref, p_ref, wg_ref, wp_ref, g_ref, gf_ref, o_ref, y_ref):
    x = x_ref[...]
    z = _dot3(_rmsnorm(x, g_ref[...]), wg_ref[...])
    x3 = x + jax.nn.sigmoid(z) * _dot3(p_ref[...], wp_ref[...])
    o_ref[...] = x3
    y_ref[...] = _rmsnorm(x3, gf_ref[...])


def _s_ple(x, p, wg, wp, g, layer, g_final):
    n_rows, d = x.shape
    pd = p.shape[-1]
    full = pl.BlockSpec((n_rows, d), lambda i: (0, 0))
    return pl.pallas_call(
        _s_ple_kernel,
        grid=(1,),
        in_specs=[full,
                  pl.BlockSpec((None, n_rows, pd), lambda i: (layer, 0, 0)),
                  pl.BlockSpec((None, d, d), lambda i: (layer, 0, 0), pipeline_mode=pl.Buffered(1)),
                  pl.BlockSpec((None, pd, d), lambda i: (layer, 0, 0), pipeline_mode=pl.Buffered(1)),
                  pl.BlockSpec((None, 1, d), lambda i: (layer, 0, 0)),
                  pl.BlockSpec((1, d), lambda i: (0, 0))],
        out_specs=[full, full],
        out_shape=[jax.ShapeDtypeStruct((n_rows, d), F32), jax.ShapeDtypeStruct((n_rows, d), F32)],
        compiler_params=_cparams(1),
        name="s_ple",
    )(x, p, wg, wp, g, g_final)


def kernel(x_prompt, x_sample, cache_k, cache_v, state_conv, page_table, p_prompt, p_sample, norm_mix, w_in, conv_w,
           w_out, norm_ffn, dense_w1, dense_w3, dense_w2, moe_router, moe_w1, moe_w3, moe_w2, norm_ple, w_ple_gate,
           w_ple_proj, norm_final):
    b, s, d = x_prompt.shape
    db, t_len, _ = x_sample.shape
    depth = w_in.shape[0]
    n_heads, page, hd = cache_k.shape[2:]
    n_pages = page_table.shape[1]
    n_experts = moe_router.shape[-1]
    ppb = BLOCK // page
    past_len = n_pages * page
    assert n_heads * hd == d and w_in.shape[-1] == N_IN_FIELDS * d and BLOCK % page == 0 and hd == LANE
    assert past_len % BLOCK == 0, "cached tail pages of the own block are not supported"
    assert s % ROW_TILE == 0 and s % page == 0 and t_len <= SUBLANE

    bs, ds = b * s, db * t_len
    tp = SUBLANE

    slopes = jnp.exp2(-8.0 * jnp.arange(1, n_heads + 1, dtype=F32) / n_heads)
    w_out_b = w_out.astype(BF16)
    wg_b, wp_b = w_ple_gate.astype(BF16), w_ple_proj.astype(BF16)
    n_exp_pad = 2 * SUBLANE
    assert n_experts <= n_exp_pad
    router_t = jnp.pad(moe_router.transpose(0, 2, 1), ((0, 0), (0, n_exp_pad - n_experts), (0, 0)))
    r_hi = router_t.astype(BF16)
    r_lo = (router_t - r_hi.astype(F32)).astype(BF16)
    router_cat = jnp.concatenate([r_hi, r_hi, r_lo], axis=-1)
    g_mix, g_ffn, g_ple = (g.reshape(depth, 1, d) for g in (norm_mix, norm_ffn, norm_ple))
    g_final = norm_final.reshape(1, d)

    x = x_prompt.reshape(bs, d)
    xs = x_sample.reshape(ds, d)
    p_dim = p_prompt.shape[-1]
    p_p = p_prompt.reshape(depth, bs, p_dim)
    p_s = p_sample.reshape(depth, ds, p_dim)

    means_t = _block_means(cache_k, page_table).transpose(0, 1, 3, 2, 4)
    state_t = state_conv.transpose(0, 2, 1, 3)
    batch_ix = jnp.arange(db)[:, None, None, None, None]
    page_off = jnp.arange(ppb)

    bm_moe = 3 * ROW_TILE
    bm_dense = _row_tile(bs, 4 * ROW_TILE)
    n_dense_chunks = bs // bm_dense
    dense_tables = (jnp.zeros((n_dense_chunks,), I32), jnp.arange(n_dense_chunks, dtype=I32),
                    jnp.full((n_dense_chunks,), bm_dense, I32))
    one_gate = jnp.ones((1, ds, 1), F32)

    k_all = jnp.zeros((depth, b, s // page, n_heads, page, hd), F32)
    v_all = jnp.zeros_like(k_all)
    cp, ksm, vsm, csm = [], [], [], []
    xn = _norm_rows(x, g_mix, 0)
    ys = None
    for i in range(depth):
        hs = _s_inproj(xs, g_mix, w_in, i)
        qkv = hs[:, :3 * d].reshape(db, t_len, 3, n_heads, hd).transpose(2, 0, 3, 1, 4)
        qkv_p = jnp.pad(qkv, ((0, 0), (0, 0), (0, 0), (0, tp - t_len), (0, 0)))
        picks = _sample_select(qkv_p[0], means_t, i)[:, :, :t_len, :TOPK_BLOCKS]
        picks = picks.transpose(0, 2, 1, 3)
        cols = picks[..., None] * ppb + page_off
        phys = page_table[batch_ix, cols]
        att_s = _sample_attend(qkv_p[0], qkv_p[1], qkv_p[2], cache_k, cache_v, phys.reshape(-1).astype(I32),
                               picks.reshape(-1).astype(I32), slopes, i, t_len, past_len)
        att_s = att_s[:, :, :t_len].transpose(2, 0, 1, 3).reshape(t_len, db, d)
        fields_s = hs.reshape(db, t_len, N_IN_FIELDS, d).transpose(2, 1, 0, 3)
        mix_s, new_state = _mix_sample(fields_s, att_s, state_t[i], conv_w, i)
        mix_s = mix_s.transpose(1, 0, 2).reshape(ds, d)
        if i % 2 == 0:
            x1s, xns = _s_outproj(mix_s, xs, w_out, g_ffn, i)
            x2s, fw1, fw3, fw2 = _s_swiglu(xns, dense_w1, dense_w3, dense_w2, one_gate, x1s, wsel=(i // 2,))
        else:
            x1s, xns, e_s, gate_s = _s_outproj(mix_s, xs, w_out, g_ffn, i, router_t[i // 2], n_experts)
            ex = jnp.arange(n_experts, dtype=I32)[:, None]
            gates = sum(jnp.where(e_s[k][None, :] == ex, gate_s[k][None, :], 0.0) for k in range(MOE_TOPK))
            x2s, fw1, fw3, fw2 = _s_swiglu(xns, moe_w1, moe_w3, moe_w2, gates[:, :, None], x1s, wsel=(i // 2,))
        xs, ys = _s_ple(x2s, p_s, w_ple_gate, w_ple_proj, g_ple, i, g_final)

        h = _inproj(xn, w_in, i)
        att_p, k_all, v_all = _moba_prompt(h, slopes, k_all, v_all, i, b, s, n_heads, hd)
        mix, u_tail = _mix_prompt(h, att_p, conv_w, i, b, s, d)
        if i % 2 == 0:
            x1, xf = _outproj(mix, x, w_out_b, g_ffn, i)
            x2 = _swiglu(xf, fw1, fw3, fw2, x1, *dense_tables, wsel=(), bm=bm_dense, routed=False)
        else:
            x1, xf, e_t, gate_t = _outproj(mix, x, w_out_b, g_ffn, i, router_cat[i // 2], n_experts)
            dest, tok_sorted, gate_sorted, sub_active, chunk_e, chunk_blk, chunk_rows = _route(
                e_t[:MOE_TOPK].T, gate_t[:MOE_TOPK].T, n_experts, bm_moe)
            xg = _gather_rows(xf, tok_sorted, sub_active, d // LANE)
            yg = _swiglu(xg, fw1, fw3, fw2, gate_sorted, chunk_e, chunk_blk, chunk_rows, wsel=(), bm=bm_moe,
                         routed=True)
            x2 = _combine(x1, yg, dest)
        last = i == depth - 1
        x, xn = _ple(x2, p_p, wg_b, wp_b, g_ple, i, g_final if last else g_mix[i + 1], F32 if last else BF16)

        cp.append(u_tail[:, SUBLANE - (CONV_W - 1):, :])
        ksm.append(qkv[1]); vsm.append(qkv[2])
        csm.append(new_state.transpose(1, 0, 2))

    y_prompt = xn.reshape(b, s, d)
    y_sample = ys.reshape(db, t_len, d)
    return (y_prompt, y_sample, k_all, v_all, jnp.stack(cp), jnp.stack(ksm), jnp.stack(vsm), jnp.stack(csm))
```

```python
import functools
import math

import jax
import jax.numpy as jnp
from jax import lax
from jax.experimental import pallas as pl
from jax.experimental.pallas import tpu as pltpu

F32 = jnp.float32
BF16 = jnp.bfloat16
I32 = jnp.int32

RMS_EPS = 1e-6
NEG = -1e30
BLOCK = 256
TOPK_BLOCKS = 3
MOE_TOPK = 2
CONV_W = 3
N_IN_FIELDS = 8
LOG2E = math.log2(math.e)

LANE = 128
SUBLANE = 8
ROW_TILE = 256
VMEM_LIMIT = 56 * 1024 * 1024
N_ALIBI_TERMS = 3
N_DMA_PRIORITIES = 2


def _cparams(n_axes, vmem=VMEM_LIMIT):
    return pltpu.CompilerParams(dimension_semantics=("arbitrary",) * n_axes, vmem_limit_bytes=vmem)


def _row_tile(rows, max_tile):
    n = rows // ROW_TILE
    best = 1
    for k in range(1, n + 1):
        if n % k == 0 and k * ROW_TILE <= max_tile:
            best = k
    return best * ROW_TILE


def _col_tile(cols, max_tile):
    best = LANE
    for t in range(LANE, min(cols, max_tile) + 1, LANE):
        if cols % t == 0:
            best = t
    return best


def _rmsnorm(x, g):
    ms = jnp.mean(x * x, axis=-1, keepdims=True)
    return x * lax.rsqrt(ms + RMS_EPS) * g


def _dot_nt(a, b):
    return lax.dot_general(a, b, (((1,), (1,)), ((), ())), preferred_element_type=F32)


def _split_bf16(x, n=2):
    terms = []
    for _ in range(n - 1):
        t = x.astype(BF16)
        terms.append(t)
        x = x - t.astype(F32)
    terms.append(x.astype(BF16))
    return terms


def _dot3(x, w):
    return _dot3_split(x, *_split_bf16(w))


def _dot3_split(x, wh, wl):
    xh, xl = _split_bf16(x)
    dot = functools.partial(jnp.dot, preferred_element_type=F32)
    m = x.shape[0]
    if m % (2 * SUBLANE):
        return dot(xh, wh) + (dot(xl, wh) + dot(xh, wl))
    y = dot(jnp.concatenate([xh, xl], axis=0), wh)
    return y[:m] + (y[m:] + dot(xh, wl))


def _dot_nt3(a, b):
    (ah, al), (bh, bl) = _split_bf16(a), _split_bf16(b)
    m = a.shape[0]
    if m % (2 * SUBLANE):
        return _dot_nt(ah, bh) + (_dot_nt(al, bh) + _dot_nt(ah, bl))
    y = _dot_nt(jnp.concatenate([ah, al], axis=0), bh)
    return y[:m] + (y[m:] + _dot_nt(ah, bl))


def _block_of(pos):
    assert BLOCK & (BLOCK - 1) == 0
    return lax.shift_right_logical(pos, BLOCK.bit_length() - 1)


def _top_picks(cand, col, n, sentinel, axis=-1):
    picks = []
    colf = col.astype(F32)
    for _ in range(n):
        m = jnp.max(cand, axis=axis, keepdims=True)
        first = jnp.min(jnp.where(cand == m, colf, float(sentinel)), axis=axis, keepdims=True)
        picks.append(first.astype(I32))
        cand = jnp.where(colf == first, -jnp.inf, cand)
    return picks


def _load_row_slab(ref, lead, r0, n_rows, sl, n_sl):
    return ref[lead + (pl.ds(r0 * n_sl + sl, n_rows, stride=n_sl), slice(None))]


def _store_row_slabs(ref, r0, val):
    n_rows, d = val.shape
    n_sl = d // LANE
    for sl in range(n_sl):
        ref[pl.ds(r0 * n_sl + sl, n_rows, stride=n_sl), :] = val[:, sl * LANE:(sl + 1) * LANE]


def _norm_rows_kernel(x_ref, g_ref, o_ref):
    o_ref[...] = _rmsnorm(x_ref[...], g_ref[...]).astype(o_ref.dtype)


def _norm_rows(x, g, layer):
    rows, d = x.shape
    tm = _row_tile(rows, 768)
    return pl.pallas_call(
        _norm_rows_kernel,
        grid=(rows // tm,),
        in_specs=[pl.BlockSpec((tm, d), lambda i: (i, 0)), pl.BlockSpec((None, 1, d), lambda i: (layer, 0, 0))],
        out_specs=pl.BlockSpec((tm, d), lambda i: (i, 0)),
        out_shape=jax.ShapeDtypeStruct((rows, d), BF16),
        compiler_params=_cparams(1),
        name="norm_rows",
    )(x, g)


def _inproj_kernel(xn_ref, w_ref, o_ref):
    o_ref[...] = jnp.dot(xn_ref[...], w_ref[...].astype(BF16), preferred_element_type=F32)


def _inproj(xn, w, layer):
    rows, d = xn.shape
    n = w.shape[-1]
    tm = _row_tile(rows, 2816)
    tn = _col_tile(n, 512)
    return pl.pallas_call(
        _inproj_kernel,
        grid=(rows // tm, n // tn),
        in_specs=[
            pl.BlockSpec((tm, d), lambda i, j: (i, 0)),
            pl.BlockSpec((None, d, tn), lambda i, j: (layer, 0, j)),
        ],
        out_specs=pl.BlockSpec((tm, tn), lambda i, j: (i, j)),
        out_shape=jax.ShapeDtypeStruct((rows, n), F32),
        compiler_params=_cparams(2),
        name="inproj",
    )(xn, w)


def _moba_prompt_kernel(slopes_ref, q_ref, k_ref, v_ref, kall_ref, vall_ref, o_ref, ko_ref, vo_ref, kx_ref, vx_ref,
                        mt_ref, pen_ref, *, nb, hd, nbp, sel_rows):
    del kall_ref, vall_ref
    s_len = nb * BLOCK
    slope2 = slopes_ref[pl.program_id(1)] * LOG2E
    qscale = hd ** -0.5 * LOG2E
    k = k_ref[...]
    v = v_ref[...]
    ko_ref[...] = k.reshape(ko_ref.shape)
    vo_ref[...] = v.reshape(vo_ref.shape)

    lane = lax.broadcasted_iota(I32, (s_len, hd), 1)
    row = lax.broadcasted_iota(I32, (s_len, hd), 0)
    a_terms = _split_bf16(slope2 * row.astype(F32), N_ALIBI_TERMS)
    kext = jnp.where(lane == _block_of(row), 1.0, 0.0).astype(BF16)
    for t, a in enumerate(a_terms):
        kext = jnp.where(lane == nb + t, a, kext)
    kx_ref[...] = jnp.concatenate([k.astype(BF16), kext], axis=1)
    vx_ref[...] = jnp.concatenate([v.astype(BF16), jnp.where(lane == 0, 1.0, 0.0).astype(BF16)], axis=1)

    means = jnp.mean(k.reshape(nb, BLOCK, hd), axis=1)
    m_hi, m_lo = _split_bf16(means)
    mt_ref[...] = jnp.zeros_like(mt_ref)
    mt_ref[0:nb, :] = jnp.concatenate([m_hi, m_hi, m_lo], axis=1)

    col = lax.broadcasted_iota(I32, (sel_rows, nbp), 1)
    rsel = lax.broadcasted_iota(I32, (sel_rows, nbp), 0)
    ones_lane = (col >= nb) & (col < nb + N_ALIBI_TERMS)

    def select(c, carry):
        row0 = pl.multiple_of(c * sel_rows, sel_rows)
        q_hi, q_lo = _split_bf16(q_ref[pl.ds(row0, sel_rows), :])
        sc = _dot_nt(jnp.concatenate([q_hi, q_lo, q_hi], axis=1), mt_ref[...])
        n_past = _block_of(row0 + rsel)
        picks = _top_picks(jnp.where(col < n_past, sc, NEG), col, TOPK_BLOCKS, nbp)
        pen = jnp.where(col == n_past, 0.0, NEG)
        for r, p in enumerate(picks):
            pen = jnp.where((col == p) & (r < n_past), 0.0, pen)
        pen_ref[pl.ds(row0, sel_rows), :] = jnp.where(ones_lane, 1.0, pen).astype(BF16)
        return carry

    lax.fori_loop(0, s_len // sel_rows, select, 0)

    r_io = lax.broadcasted_iota(I32, (BLOCK, BLOCK), 0)
    c_io = lax.broadcasted_iota(I32, (BLOCK, BLOCK), 1)

    for i in range(nb):
        row0, win = i * BLOCK, i * BLOCK
        qx = jnp.concatenate([(q_ref[row0:row0 + BLOCK, :] * qscale).astype(BF16), pen_ref[row0:row0 + BLOCK, :]], axis=1)
        s_all = _dot_nt(qx, kx_ref[0:win + BLOCK, :])
        s_own = jnp.where(r_io >= c_io, s_all[:, win:], NEG)
        s_all = jnp.concatenate([s_all[:, :win], s_own], axis=1) if i else s_own
        m = jnp.max(s_all, axis=-1, keepdims=True)
        p = jnp.exp2(s_all - m).astype(BF16)
        o2 = jnp.dot(p, vx_ref[0:win + BLOCK, :], preferred_element_type=F32)
        o_ref[row0:row0 + BLOCK, :] = o2[:, :hd] / o2[:, hd:hd + 1]


def _moba_prompt(h, slopes, k_all, v_all, layer, b, s, n_heads, hd):
    assert s % BLOCK == 0 and TOPK_BLOCKS == 3
    nb = s // BLOCK
    nbp = hd
    assert nb + N_ALIBI_TERMS <= nbp
    sel_rows = min(s, 1024)
    assert s % sel_rows == 0
    n_pg, page = k_all.shape[2], k_all.shape[4]
    kern = functools.partial(_moba_prompt_kernel, nb=nb, hd=hd, nbp=nbp, sel_rows=sel_rows)
    pages = jax.ShapeDtypeStruct(k_all.shape, F32)
    page_spec = pl.BlockSpec((None, None, n_pg, None, page, hd), lambda bi, hi: (layer, bi, 0, hi, 0, 0))
    anyspec = pl.BlockSpec(memory_space=pl.ANY)
    return pl.pallas_call(
        kern,
        grid=(b, n_heads),
        in_specs=[
            pl.BlockSpec(memory_space=pltpu.SMEM),
            pl.BlockSpec((s, hd), lambda bi, hi: (bi, hi)),
            pl.BlockSpec((s, hd), lambda bi, hi: (bi, n_heads + hi)),
            pl.BlockSpec((s, hd), lambda bi, hi: (bi, 2 * n_heads + hi)),
            anyspec, anyspec,
        ],
        out_specs=[pl.BlockSpec((s, hd), lambda bi, hi: (bi, hi)), page_spec, page_spec],
        out_shape=[jax.ShapeDtypeStruct((b * s, n_heads * hd), F32), pages, pages],
        input_output_aliases={4: 1, 5: 2},
        scratch_shapes=[
            pltpu.VMEM((s, 2 * hd), BF16),
            pltpu.VMEM((s, 2 * hd), BF16),
            pltpu.VMEM((nbp, 3 * hd), BF16),
            pltpu.VMEM((s, nbp), BF16),
        ],
        compiler_params=_cparams(2),
        name="moba_prompt",
    )(slopes, h, h, h, k_all, v_all)


def _block_means_kernel(pt_ref, *refs, ppb, page, bps):
    del pt_ref
    o_ref = refs[ppb * bps]
    for blk in range(bps):
        tot = jnp.sum(refs[blk * ppb][0, 0], axis=1)
        for r in refs[blk * ppb + 1:(blk + 1) * ppb]:
            tot = tot + jnp.sum(r[0, 0], axis=1)
        o_ref[0, 0, blk] = tot * (1.0 / (ppb * page))


def _block_means(cache_k, page_table):
    depth, _, n_heads, page, hd = cache_k.shape
    db, n_pages = page_table.shape
    ppb = BLOCK // page
    nbk = n_pages // ppb
    bps = next(k for k in (4, 2, 1) if nbk % k == 0)
    pt = page_table.reshape(-1)

    def page_map(jp, l, b, n, pt_ref):
        return (l, pt_ref[b * n_pages + n * (ppb * bps) + jp], 0, 0, 0)

    return pl.pallas_call(
        functools.partial(_block_means_kernel, ppb=ppb, page=page, bps=bps),
        grid_spec=pltpu.PrefetchScalarGridSpec(
            num_scalar_prefetch=1,
            grid=(depth, db, nbk // bps),
            in_specs=[pl.BlockSpec((1, 1, n_heads, page, hd), functools.partial(page_map, jp))
                      for jp in range(ppb * bps)],
            out_specs=pl.BlockSpec((1, 1, bps, n_heads, hd), lambda l, b, n, pt_ref: (l, b, n, 0, 0)),
        ),
        out_shape=jax.ShapeDtypeStruct((depth, db, nbk, n_heads, hd), F32),
        compiler_params=_cparams(3),
        name="block_means",
    )(pt, *([cache_k] * (ppb * bps)))


def _sample_select_kernel(q_ref, m_ref, o_ref, *, nbk):
    q = q_ref[0]
    means = m_ref[0, 0]
    sc = lax.dot_general(q, means, (((2,), (2,)), ((0,), (0,))),
                         precision=lax.Precision.HIGHEST, preferred_element_type=F32)
    col = lax.broadcasted_iota(I32, sc.shape, 2)
    picks = _top_picks(sc, col, TOPK_BLOCKS, nbk)
    lane = lax.broadcasted_iota(I32, o_ref.shape[1:], 2)
    o_ref[0] = jnp.where(lane == 0, picks[0], jnp.where(lane == 1, picks[1], picks[2]))


def _sample_select(q_s, means_t, layer):
    db, n_heads, tp, hd = q_s.shape
    nbk = means_t.shape[3]
    assert nbk >= TOPK_BLOCKS
    return pl.pallas_call(
        functools.partial(_sample_select_kernel, nbk=nbk),
        grid=(db,),
        in_specs=[
            pl.BlockSpec((1, n_heads, tp, hd), lambda b: (b, 0, 0, 0)),
            pl.BlockSpec((1, 1, n_heads, nbk, hd), lambda b: (layer, b, 0, 0, 0)),
        ],
        out_specs=pl.BlockSpec((1, n_heads, tp, LANE), lambda b: (b, 0, 0, 0)),
        out_shape=jax.ShapeDtypeStruct((db, n_heads, tp, LANE), I32),
        compiler_params=_cparams(1),
        name="sample_select",
    )(q_s, means_t)


def _sample_attend_kernel(phys_ref, blk_ref, slopes_ref, q_ref, kn_ref, vn_ref, ck_hbm, cv_hbm, o_ref, kbuf, vbuf, sems,
                          *, layer, t_len, n_sel, past_len, hd, page):
    b, hh = pl.program_id(0), pl.program_id(1)
    n_heads = pl.num_programs(1)
    step = b * n_heads + hh
    n_steps = pl.num_programs(0) * n_heads

    def page_copies(st, slot):
        sb, sh = st // n_heads, st % n_heads
        cps = []
        for t in range(t_len):
            for c in range(n_sel):
                pg = phys_ref[((sb * t_len + t) * n_heads + sh) * n_sel + c]
                j = t * n_sel + c
                cps.append(pltpu.make_async_copy(ck_hbm.at[layer, pg, sh], kbuf.at[slot, j], sems.at[0, slot]))
                cps.append(pltpu.make_async_copy(cv_hbm.at[layer, pg, sh], vbuf.at[slot, j], sems.at[1, slot]))
        return cps

    @pl.when(step == 0)
    def _():
        for cp in page_copies(step, 0):
            cp.start()

    @pl.when(step + 1 < n_steps)
    def _():
        for cp in page_copies(step + 1, (step + 1) % 2):
            cp.start()

    slot = step % 2
    for cp in page_copies(step, slot):
        cp.wait()

    slope = slopes_ref[hh]
    scale = hd ** -0.5
    ppb = BLOCK // page
    qf = q_ref[0, 0] * scale
    tp = qf.shape[0]
    s_own_all = _dot_nt3(qf, kn_ref[0, 0])
    vn = vn_ref[0, 0]
    lane = lax.broadcasted_iota(I32, (1, page), 1)
    tcol = lax.broadcasted_iota(I32, (1, tp), 1)
    outs = []
    for t in range(t_len):
        tq = past_len + t
        pos = jnp.concatenate(
            [blk_ref[((b * t_len + t) * n_heads + hh) * TOPK_BLOCKS + c // ppb] * BLOCK + (c % ppb) * page + lane
             for c in range(n_sel)], axis=1)
        k_sel = kbuf[slot, t * n_sel:(t + 1) * n_sel].reshape(n_sel * page, hd)
        v_sel = vbuf[slot, t * n_sel:(t + 1) * n_sel].reshape(n_sel * page, hd)
        s_sel = _dot_nt3(qf, k_sel)[t:t + 1, :] - slope * (tq - pos).astype(F32)
        s_own = s_own_all[t:t + 1, :] - slope * (t - tcol).astype(F32)
        s_own = jnp.where(tcol <= t, s_own, NEG)
        m = jnp.maximum(jnp.max(s_own, axis=-1, keepdims=True), jnp.max(s_sel, axis=-1, keepdims=True))
        p_own = jnp.exp(s_own - m)
        p_sel = jnp.exp(s_sel - m)
        l = jnp.sum(p_own, axis=-1, keepdims=True) + jnp.sum(p_sel, axis=-1, keepdims=True)
        acc = _dot3(jnp.broadcast_to(p_own, (tp, tp)), vn) + _dot3(jnp.broadcast_to(p_sel, (tp, n_sel * page)), v_sel)
        outs.append(acc[0:1, :] / l)
    outs += [jnp.zeros((1, hd), F32)] * (tp - t_len)
    o_ref[0, 0] = jnp.concatenate(outs, axis=0)


def _sample_attend(q_s, k_new, v_new, cache_k, cache_v, phys, blk, slopes, layer, t_len, past_len):
    db, n_heads, tp, hd = q_s.shape
    page = cache_k.shape[3]
    n_sel = TOPK_BLOCKS * (BLOCK // page)
    n_pg = t_len * n_sel
    small = pl.BlockSpec((1, 1, tp, hd), lambda b, hh, phys_ref, blk_ref: (b, hh, 0, 0))
    anyspec = pl.BlockSpec(memory_space=pl.ANY)
    kern = functools.partial(_sample_attend_kernel, layer=layer, t_len=t_len, n_sel=n_sel, past_len=past_len, hd=hd,
                             page=page)
    return pl.pallas_call(
        kern,
        grid_spec=pltpu.PrefetchScalarGridSpec(
            num_scalar_prefetch=2,
            grid=(db, n_heads),
            in_specs=[pl.BlockSpec(memory_space=pltpu.SMEM), small, small, small, anyspec, anyspec],
            out_specs=small,
            scratch_shapes=[pltpu.VMEM((2, n_pg, page, hd), F32), pltpu.VMEM((2, n_pg, page, hd), F32),
                            pltpu.SemaphoreType.DMA((2, 2))],
        ),
        out_shape=jax.ShapeDtypeStruct((db, n_heads, tp, hd), F32),
        compiler_params=_cparams(2),
        name="sample_attend",
    )(phys, blk, slopes, q_s, k_new, v_new, cache_k, cache_v)


def _mix(gb, ga, gcv, att, u0, u1, u2, cw):
    conv = cw[0:1] * u0 + cw[1:2] * u1 + cw[2:3] * u2
    return jax.nn.sigmoid(ga) * att + jax.nn.sigmoid(gcv) * (gb * conv)


def _mix_prompt_kernel(gb_ref, gc_ref, xc_ref, ga_ref, gcv_ref, gch_ref, xch_ref, att_ref, cw_ref, o_ref, us_ref,
                       *, tiles_per_seq, n_tiles):
    i = pl.program_id(0)

    @pl.when(i >= n_tiles)
    def _():
        o_ref[...] = jnp.zeros_like(o_ref)

    @pl.when(i < n_tiles)
    def _():
        cw = cw_ref[...]
        u = gc_ref[...] * xc_ref[...]
        halo = jnp.where(i % tiles_per_seq == 0, 0.0, gch_ref[...] * xch_ref[...])
        u1 = pltpu.roll(u, 1, 0)
        u2 = pltpu.roll(u, 2, 0)
        o_ref[...] = _mix(gb_ref[...], ga_ref[...], gcv_ref[...], att_ref[...], u2, u1, u, cw).astype(o_ref.dtype)
        row = lax.broadcasted_iota(I32, (SUBLANE, u.shape[1]), 0)
        f1 = jnp.where(row < 1, pltpu.roll(halo, 1, 0), u1[0:SUBLANE])
        f2 = jnp.where(row < 2, pltpu.roll(halo, 2, 0), u2[0:SUBLANE])
        o_ref[0:SUBLANE, :] = _mix(gb_ref[0:SUBLANE, :], ga_ref[0:SUBLANE, :], gcv_ref[0:SUBLANE, :],
                                   att_ref[0:SUBLANE, :], f2, f1, u[0:SUBLANE], cw).astype(o_ref.dtype)
        us_ref[...] = u[u.shape[0] - SUBLANE:, :]


def _mix_prompt(h, att, conv_w, layer, b, s, d):
    assert CONV_W == 3
    tm = ROW_TILE
    tps = s // tm
    hpt = tm // SUBLANE
    n_tiles = b * tps
    rows = h.shape[0]
    clamp = lambda i: jnp.minimum(i, n_tiles - 1)

    def field(f):
        return pl.BlockSpec((tm, d), lambda i: (clamp(i), f))

    def halo(f):
        return pl.BlockSpec((SUBLANE, d), lambda i: (jnp.maximum(clamp(i) * hpt - 1, 0), f))

    return pl.pallas_call(
        functools.partial(_mix_prompt_kernel, tiles_per_seq=tps, n_tiles=n_tiles),
        grid=(rows // tm,),
        in_specs=[field(3), field(4), field(5), field(6), field(7), halo(4), halo(5),
                  pl.BlockSpec((tm, d), lambda i: (clamp(i), 0)),
                  pl.BlockSpec((None, CONV_W, d), lambda i: (layer, 0, 0))],
        out_specs=[pl.BlockSpec((tm, d), lambda i: (i, 0)),
                   pl.BlockSpec((None, SUBLANE, d), lambda i: (clamp(i) // tps, 0, 0))],
        out_shape=[jax.ShapeDtypeStruct((rows, d), BF16), jax.ShapeDtypeStruct((b, SUBLANE, d), F32)],
        compiler_params=_cparams(1),
        name="mix_prompt",
    )(h, h, h, h, h, h, h, att, conv_w)


def _mix_sample_kernel(f_ref, att_ref, st_ref, cw_ref, o_ref, ns_ref, *, t_len):
    cw = cw_ref[...]
    u = [st_ref[0], st_ref[1]] + [f_ref[4, t] * f_ref[5, t] for t in range(t_len)]
    for t in range(t_len):
        o_ref[t] = _mix(f_ref[3, t], f_ref[6, t], f_ref[7, t], att_ref[t], u[t], u[t + 1], u[t + 2], cw).astype(o_ref.dtype)
    ns_ref[0] = u[t_len]
    ns_ref[1] = u[t_len + 1]


def _mix_sample(fields, att, state, conv_w, layer):
    _, t_len, db, d = fields.shape
    return pl.pallas_call(
        functools.partial(_mix_sample_kernel, t_len=t_len),
        grid=(1,),
        in_specs=[pl.BlockSpec(fields.shape, lambda i: (0, 0, 0, 0)),
                  pl.BlockSpec(att.shape, lambda i: (0, 0, 0)),
                  pl.BlockSpec(state.shape, lambda i: (0, 0, 0)),
                  pl.BlockSpec((None, CONV_W, d), lambda i: (layer, 0, 0))],
        out_specs=[pl.BlockSpec((t_len, db, d), lambda i: (0, 0, 0)),
                   pl.BlockSpec((CONV_W - 1, db, d), lambda i: (0, 0, 0))],
        out_shape=[jax.ShapeDtypeStruct((t_len, db, d), F32), jax.ShapeDtypeStruct((CONV_W - 1, db, d), F32)],
        compiler_params=_cparams(1),
        name="mix_sample",
    )(fields, att, state, conv_w)


def _outproj_kernel(*refs, n_experts):
    if n_experts:
        mix_ref, x_ref, w_ref, g_ref, r_ref, x1_ref, xnf_ref, e_ref, gt_ref = refs
    else:
        mix_ref, x_ref, w_ref, g_ref, x1_ref, xn_ref = refs
    x1 = x_ref[...] + jnp.dot(mix_ref[...], w_ref[...], preferred_element_type=F32)
    x1_ref[...] = x1
    xn = _rmsnorm(x1, g_ref[...])
    if not n_experts:
        xn_ref[...] = xn.astype(BF16)
        return
    _store_row_slabs(xnf_ref, 0, xn)
    x_hi, x_lo = _split_bf16(xn)
    logits = _dot_nt(r_ref[...], jnp.concatenate([x_hi, x_lo, x_hi], axis=1))
    ex = lax.broadcasted_iota(I32, logits.shape, 0)
    logits = jnp.where(ex < n_experts, logits, -jnp.inf)
    (i1, i2) = _top_picks(logits, ex, MOE_TOPK, logits.shape[0], axis=0)
    m1 = jnp.max(logits, axis=0, keepdims=True)
    m2 = jnp.max(jnp.where(ex == i1, -jnp.inf, logits), axis=0, keepdims=True)
    e2 = jnp.exp(m2 - m1)
    den = 1.0 + e2
    orow = lax.broadcasted_iota(I32, e_ref.shape, 0)
    e_ref[...] = jnp.where(orow == 0, i1, i2)
    gt_ref[...] = jnp.where(orow == 0, 1.0 / den, e2 / den)


def _outproj(mix, x, w, g, layer, router_t=None, n_experts=0):
    rows, d = x.shape
    tm = _row_tile(rows, 256)
    rowspec = pl.BlockSpec((tm, d), lambda i: (i, 0))
    in_specs = [rowspec, rowspec,
                pl.BlockSpec((None, d, d), lambda i: (layer, 0, 0), pipeline_mode=pl.Buffered(1)),
                pl.BlockSpec((None, 1, d), lambda i: (layer, 0, 0))]
    args = [mix, x, w, g]
    if n_experts:
        tspec = pl.BlockSpec((SUBLANE, tm), lambda i: (0, i))
        in_specs.append(pl.BlockSpec(router_t.shape, lambda i: (0, 0)))
        out_specs = [rowspec, pl.BlockSpec((tm * (d // LANE), LANE), lambda i: (i, 0)), tspec, tspec]
        out_shape = [jax.ShapeDtypeStruct((rows, d), F32), jax.ShapeDtypeStruct((rows * (d // LANE), LANE), F32),
                     jax.ShapeDtypeStruct((SUBLANE, rows), I32), jax.ShapeDtypeStruct((SUBLANE, rows), F32)]
        args.append(router_t)
    else:
        out_specs = [rowspec, rowspec]
        out_shape = [jax.ShapeDtypeStruct((rows, d), F32), jax.ShapeDtypeStruct((rows, d), BF16)]
    return pl.pallas_call(
        functools.partial(_outproj_kernel, n_experts=n_experts),
        grid=(rows // tm,),
        in_specs=in_specs, out_specs=out_specs, out_shape=out_shape,
        compiler_params=_cparams(1),
        name="outproj",
    )(*args)


def _swiglu_kernel(ce_ref, cb_ref, cr_ref, x_ref, w1_ref, w3_ref, w2_ref, s_ref, o_ref, acc_ref, *xb_scratch,
                   n_f, bm, routed):
    del ce_ref, cb_ref
    c, f = pl.program_id(0), pl.program_id(1)
    rows = cr_ref[c]
    nsub = (rows + (ROW_TILE - 1)) // ROW_TILE
    n_sl = acc_ref.shape[1] // LANE

    if routed:
        xb_ref, = xb_scratch

        @pl.when(f == 0)
        def _():
            def conv(sb, carry):
                r0 = pl.multiple_of(sb * ROW_TILE, ROW_TILE)
                for sl in range(n_sl):
                    xb_ref[pl.ds(r0, ROW_TILE), sl * LANE:(sl + 1) * LANE] = _load_row_slab(
                        x_ref, (), r0, ROW_TILE, sl, n_sl).astype(BF16)
                return carry

            lax.fori_loop(0, nsub, conv, 0)
    else:
        xb_ref = x_ref

    def accumulate(r0, n_rows):
        xs = xb_ref[pl.ds(r0, n_rows), :]
        h1 = jnp.dot(xs, w1_ref[...], preferred_element_type=F32)
        h3 = jnp.dot(xs, w3_ref[...], preferred_element_type=F32)
        hh = (jax.nn.silu(h1) * h3).astype(BF16)
        part = jnp.dot(hh, w2_ref[...], preferred_element_type=F32)

        @pl.when(f == 0)
        def _():
            acc_ref[pl.ds(r0, n_rows), :] = part

        @pl.when(f > 0)
        def _():
            acc_ref[pl.ds(r0, n_rows), :] += part

    @pl.when(nsub == bm // ROW_TILE)
    def _():
        accumulate(0, bm)

    @pl.when(nsub < bm // ROW_TILE)
    def _():
        def sub(sb, carry):
            accumulate(pl.multiple_of(sb * ROW_TILE, ROW_TILE), ROW_TILE)
            return carry

        lax.fori_loop(0, nsub, sub, 0)

    @pl.when(f == n_f - 1)
    def _():
        def fin(sb, carry):
            r0 = pl.multiple_of(sb * ROW_TILE, ROW_TILE)
            a = acc_ref[pl.ds(r0, ROW_TILE), :]
            if routed:
                _store_row_slabs(o_ref, r0, a * s_ref[pl.ds(r0, ROW_TILE), :])
            else:
                o_ref[pl.ds(r0, ROW_TILE), :] = s_ref[pl.ds(r0, ROW_TILE), :] + a
            return carry

        lax.fori_loop(0, nsub, fin, 0)

        def blank(sb, carry):
            rep = o_ref.shape[0] // bm
            r0 = pl.multiple_of(sb * ROW_TILE * rep, ROW_TILE * rep)
            o_ref[pl.ds(r0, ROW_TILE * rep), :] = jnp.zeros((ROW_TILE * rep, o_ref.shape[1]), F32)
            return carry

        lax.fori_loop(nsub, bm // ROW_TILE, blank, 0)


def _swiglu(x, w1, w3, w2, side, chunk_e, chunk_blk, chunk_rows, wsel, bm, routed):
    d = w1.shape[-2]
    ff = w1.shape[-1]
    tf = _col_tile(ff, 1024 if routed else 512)
    n_f = ff // tf
    n_chunks = chunk_e.shape[0]
    rep = d // LANE if routed else 1
    lead = len(wsel)
    per_expert = w1.ndim == lead + 3

    def fidx(c, f, cr):
        return jnp.where(cr[c] > 0, f, n_f - 1)

    def w13_map(c, f, ce, cb, cr):
        return wsel + ((ce[c],) if per_expert else ()) + (0, fidx(c, f, cr))

    def w2_map(c, f, ce, cb, cr):
        return wsel + ((ce[c],) if per_expert else ()) + (fidx(c, f, cr), 0)

    nlead = lead + (1 if per_expert else 0)
    w13_spec = pl.BlockSpec((None,) * nlead + (d, tf), w13_map)
    w2_spec = pl.BlockSpec((None,) * nlead + (tf, d), w2_map)
    once = dict(pipeline_mode=pl.Buffered(1))
    x_spec = pl.BlockSpec((bm * rep, x.shape[1]), lambda c, f, ce, cb, cr: (cb[c], 0), **(once if routed else {}))
    o_spec = pl.BlockSpec((bm * rep, x.shape[1]), lambda c, f, ce, cb, cr: (c, 0), **once)
    side_spec = pl.BlockSpec((bm, side.shape[1]), lambda c, f, ce, cb, cr: (cb[c], 0), **({} if routed else once))
    scratch = [pltpu.VMEM((bm, d), F32)] + ([pltpu.VMEM((bm, d), BF16)] if routed else [])
    return pl.pallas_call(
        functools.partial(_swiglu_kernel, n_f=n_f, bm=bm, routed=routed),
        grid_spec=pltpu.PrefetchScalarGridSpec(
            num_scalar_prefetch=3,
            grid=(n_chunks, n_f),
            in_specs=[x_spec, w13_spec, w13_spec, w2_spec, side_spec],
            out_specs=o_spec,
            scratch_shapes=scratch,
        ),
        out_shape=jax.ShapeDtypeStruct(x.shape, F32),
        compiler_params=_cparams(2),
        name="swiglu",
    )(chunk_e, chunk_blk, chunk_rows, x, w1, w3, w2, side)


def _gather_rows_kernel(tok_ref, act_ref, x_hbm, o_ref, sem, *, n_sl):
    sb = pl.program_id(0)
    base = sb * ROW_TILE

    def row(ref, r):
        return ref.at[pl.ds(pl.multiple_of(r * n_sl, n_sl), n_sl)]

    @pl.when(act_ref[sb] > 0)
    def _():
        def start(rp, carry):
            for k in range(N_DMA_PRIORITIES):
                r = rp * N_DMA_PRIORITIES + k
                pltpu.make_async_copy(row(x_hbm, tok_ref[base + r]), row(o_ref, r), sem).start(priority=k)
            return carry

        lax.fori_loop(0, ROW_TILE // N_DMA_PRIORITIES, start, 0)

        def wait(r, carry):
            pltpu.make_async_copy(row(x_hbm, 0), row(o_ref, r), sem).wait()
            return carry

        lax.fori_loop(0, ROW_TILE, wait, 0)

    @pl.when(act_ref[sb] == 0)
    def _():
        o_ref[...] = jnp.zeros_like(o_ref)


def _gather_rows(x, tok_sorted, sub_active, n_sl):
    n_out = tok_sorted.shape[0]
    return pl.pallas_call(
        functools.partial(_gather_rows_kernel, n_sl=n_sl),
        grid_spec=pltpu.PrefetchScalarGridSpec(
            num_scalar_prefetch=2,
            grid=(n_out // ROW_TILE,),
            in_specs=[pl.BlockSpec(memory_space=pl.ANY)],
            out_specs=pl.BlockSpec((ROW_TILE * n_sl, x.shape[1]), lambda sb, tok_ref, act_ref: (sb, 0)),
            scratch_shapes=[pltpu.SemaphoreType.DMA(())],
        ),
        out_shape=jax.ShapeDtypeStruct((n_out * n_sl, x.shape[1]), x.dtype),
        compiler_params=_cparams(1),
        name="gather_rows",
    )(tok_sorted, sub_active, x)


def _combine_kernel(dest_ref, x_ref, y_hbm, o_ref, buf_ref, sem):
    i = pl.program_id(0)
    tm, d = x_ref.shape
    n_sl = d // LANE
    base = i * tm * MOE_TOPK

    def row(ref, r):
        return ref.at[pl.ds(pl.multiple_of(r * n_sl, n_sl), n_sl)]

    def start(r, carry):
        for k in range(MOE_TOPK):
            pltpu.make_async_copy(row(y_hbm, dest_ref[base + r * MOE_TOPK + k]), row(buf_ref.at[k], r),
                                  sem).start(priority=k % N_DMA_PRIORITIES)
        return carry

    lax.fori_loop(0, tm, start, 0)

    def wait(r, carry):
        for k in range(MOE_TOPK):
            pltpu.make_async_copy(row(y_hbm, 0), row(buf_ref.at[k], r), sem).wait()
        return carry

    lax.fori_loop(0, tm, wait, 0)
    for sl in range(n_sl):
        f = _load_row_slab(buf_ref, (0,), 0, tm, sl, n_sl)
        for k in range(1, MOE_TOPK):
            f = f + _load_row_slab(buf_ref, (k,), 0, tm, sl, n_sl)
        o_ref[:, sl * LANE:(sl + 1) * LANE] = x_ref[:, sl * LANE:(sl + 1) * LANE] + f


def _combine(x, y, dest):
    rows, d = x.shape
    tm = ROW_TILE
    rowspec = pl.BlockSpec((tm, d), lambda i, dest_ref: (i, 0))
    return pl.pallas_call(
        _combine_kernel,
        grid_spec=pltpu.PrefetchScalarGridSpec(
            num_scalar_prefetch=1,
            grid=(rows // tm,),
            in_specs=[rowspec, pl.BlockSpec(memory_space=pl.ANY)],
            out_specs=rowspec,
            scratch_shapes=[pltpu.VMEM((MOE_TOPK, tm * (d // LANE), y.shape[1]), F32), pltpu.SemaphoreType.DMA(())],
        ),
        out_shape=jax.ShapeDtypeStruct(x.shape, F32),
        compiler_params=_cparams(1),
        name="combine",
    )(dest, x, y)


def _route(e_idx, gates, n_experts, bm):
    rows = e_idx.shape[0]
    ns = rows * MOE_TOPK
    n_chunks = ns // bm + n_experts
    flat_e = e_idx.reshape(-1)
    onehot = (flat_e[:, None] == jnp.arange(n_experts, dtype=I32)[None, :]).astype(I32)
    csum = jnp.cumsum(onehot, axis=0)
    rank = jnp.take_along_axis(csum, flat_e[:, None], axis=1)[:, 0] - 1
    counts = csum[-1]
    padded = (counts + bm - 1) // bm * bm
    end_pad = jnp.cumsum(padded)
    start_pad = end_pad - padded
    dest = (start_pad[flat_e] + rank).astype(I32)
    n_used = end_pad[-1] // bm
    cidx = jnp.minimum(jnp.arange(n_chunks, dtype=I32), n_used - 1)
    chunk_e = jnp.minimum(jnp.sum((end_pad[None, :] <= (cidx * bm)[:, None]).astype(I32), axis=1), n_experts - 1)
    rows_left = counts[chunk_e] - (cidx * bm - start_pad[chunk_e])
    chunk_rows = jnp.where(jnp.arange(n_chunks) < n_used, jnp.clip(rows_left, 0, bm), 0).astype(I32)
    n_buf = n_chunks * bm
    tok_sorted = jnp.zeros((n_buf,), I32).at[dest].set(jnp.arange(ns, dtype=I32) // MOE_TOPK)
    gate_sorted = jnp.zeros((n_buf,), F32).at[dest].set(gates.reshape(-1))
    sub_per = bm // ROW_TILE
    sub_in_chunk = jnp.arange(n_chunks * sub_per, dtype=I32) % sub_per
    sub_active = (sub_in_chunk * ROW_TILE < jnp.repeat(chunk_rows, sub_per)).astype(I32)
    return dest, tok_sorted, gate_sorted[:, None], sub_active, chunk_e.astype(I32), cidx.astype(I32), chunk_rows


def _ple_kernel(x_ref, p_ref, wg_ref, wp_ref, g_ref, gn_ref, o_ref, n_ref):
    x = x_ref[...]
    z = jnp.dot(_rmsnorm(x, g_ref[...]).astype(BF16), wg_ref[...], preferred_element_type=F32)
    pp = jnp.dot(p_ref[...].astype(BF16), wp_ref[...], preferred_element_type=F32)
    x3 = x + jax.nn.sigmoid(z) * pp
    o_ref[...] = x3
    n_ref[...] = _rmsnorm(x3, gn_ref[...]).astype(n_ref.dtype)


def _ple(x, p, wg, wp, g, layer, g_next, next_dtype):
    rows, d = x.shape
    pd = p.shape[-1]
    tm = _row_tile(rows, 256)
    rowspec = pl.BlockSpec((tm, d), lambda i: (i, 0))
    return pl.pallas_call(
        _ple_kernel,
        grid=(rows // tm,),
        in_specs=[rowspec,
                  pl.BlockSpec((None, tm, pd), lambda i: (layer, i, 0)),
                  pl.BlockSpec((None, d, d), lambda i: (layer, 0, 0), pipeline_mode=pl.Buffered(1)),
                  pl.BlockSpec((None, pd, d), lambda i: (layer, 0, 0), pipeline_mode=pl.Buffered(1)),
                  pl.BlockSpec((None, 1, d), lambda i: (layer, 0, 0)),
                  pl.BlockSpec((1, d), lambda i: (0, 0))],
        out_specs=[rowspec, rowspec],
        out_shape=[jax.ShapeDtypeStruct((rows, d), F32), jax.ShapeDtypeStruct((rows, d), next_dtype)],
        compiler_params=_cparams(1),
        name="ple",
    )(x, p, wg, wp, g, g_next)


def _s_inproj_kernel(x_ref, g_ref, w_ref, o_ref):
    o_ref[...] = _dot3(_rmsnorm(x_ref[...], g_ref[...]), w_ref[...])


def _s_inproj(x, g, w, layer):
    n_rows, d = x.shape
    n = w.shape[-1]
    tn = _col_tile(n, 1024)
    return pl.pallas_call(
        _s_inproj_kernel,
        grid=(n // tn,),
        in_specs=[pl.BlockSpec((n_rows, d), lambda j: (0, 0)),
                  pl.BlockSpec((None, 1, d), lambda j: (layer, 0, 0)),
                  pl.BlockSpec((None, d, tn), lambda j: (layer, 0, j))],
        out_specs=pl.BlockSpec((n_rows, tn), lambda j: (0, j)),
        out_shape=jax.ShapeDtypeStruct((n_rows, n), F32),
        compiler_params=_cparams(1),
        name="s_inproj",
    )(x, g, w)


def _s_outproj_kernel(*refs, n_experts):
    if n_experts:
        mix_ref, x_ref, w_ref, g_ref, r_ref, x1_ref, xn_ref, e_ref, gt_ref = refs
    else:
        mix_ref, x_ref, w_ref, g_ref, x1_ref, xn_ref = refs
    x1 = x_ref[...] + _dot3(mix_ref[...], w_ref[...])
    x1_ref[...] = x1
    xn = _rmsnorm(x1, g_ref[...])
    xn_ref[...] = xn
    if n_experts:
        logits = _dot_nt3(r_ref[...], xn)
        ex = lax.broadcasted_iota(I32, logits.shape, 0)
        logits = jnp.where(ex < n_experts, logits, -jnp.inf)
        (i1, i2) = _top_picks(logits, ex, MOE_TOPK, logits.shape[0], axis=0)
        m1 = jnp.max(logits, axis=0, keepdims=True)
        m2 = jnp.max(jnp.where(ex == i1, -jnp.inf, logits), axis=0, keepdims=True)
        e2 = jnp.exp(m2 - m1)
        den = 1.0 + e2
        orow = lax.broadcasted_iota(I32, e_ref.shape, 0)
        e_ref[...] = jnp.where(orow == 0, i1, i2)
        gt_ref[...] = jnp.where(orow == 0, 1.0 / den, e2 / den)


def _s_outproj(mix, x, w, g, layer, router_t=None, n_experts=0):
    n_rows, d = x.shape
    full = pl.BlockSpec((n_rows, d), lambda i: (0, 0))
    in_specs = [full, full,
                pl.BlockSpec((None, d, d), lambda i: (layer, 0, 0), pipeline_mode=pl.Buffered(1)),
                pl.BlockSpec((None, 1, d), lambda i: (layer, 0, 0))]
    out_specs = [full, full]
    out_shape = [jax.ShapeDtypeStruct((n_rows, d), F32), jax.ShapeDtypeStruct((n_rows, d), F32)]
    args = [mix, x, w, g]
    if n_experts:
        tspec = pl.BlockSpec((SUBLANE, n_rows), lambda i: (0, 0))
        in_specs.append(pl.BlockSpec(router_t.shape, lambda i: (0, 0)))
        out_specs += [tspec, tspec]
        out_shape += [jax.ShapeDtypeStruct((SUBLANE, n_rows), I32), jax.ShapeDtypeStruct((SUBLANE, n_rows), F32)]
        args.append(router_t)
    return pl.pallas_call(
        functools.partial(_s_outproj_kernel, n_experts=n_experts),
        grid=(1,),
        in_specs=in_specs, out_specs=out_specs, out_shape=out_shape,
        compiler_params=_cparams(1),
        name="s_outproj",
    )(*args)


def _s_swiglu_kernel(x_ref, w1_ref, w3_ref, w2_ref, gate_ref, res_ref, o_ref, w1b_ref, w3b_ref, w2b_ref, acc_ref):
    e, f = pl.program_id(0), pl.program_id(1)
    (w1h, w1l), (w3h, w3l), (w2h, w2l) = (_split_bf16(r[...]) for r in (w1_ref, w3_ref, w2_ref))
    w1b_ref[...], w3b_ref[...], w2b_ref[...] = w1h, w3h, w2h

    @pl.when((e == 0) & (f == 0))
    def _():
        acc_ref[...] = jnp.zeros_like(acc_ref)

    xs = x_ref[...]
    hh = jax.nn.silu(_dot3_split(xs, w1h, w1l)) * _dot3_split(xs, w3h, w3l)
    gate = gate_ref[...]
    acc_ref[...] += jnp.where(gate != 0.0, gate * _dot3_split(hh, w2h, w2l), 0.0)

    @pl.when((e == pl.num_programs(0) - 1) & (f == pl.num_programs(1) - 1))
    def _():
        o_ref[...] = res_ref[...] + acc_ref[...]


def _s_swiglu(x, w1, w3, w2, gates, res, wsel):
    n_rows, d = x.shape
    ff = w1.shape[-1]
    tf = _col_tile(ff, 512)
    n_e = gates.shape[0]
    per_expert = w1.ndim == len(wsel) + 3
    lead = (None,) * (len(wsel) + (1 if per_expert else 0))
    esel = (lambda e: (e,)) if per_expert else (lambda e: ())
    full = pl.BlockSpec((n_rows, d), lambda e, f: (0, 0))
    w13_spec = pl.BlockSpec(lead + (d, tf), lambda e, f: wsel + esel(e) + (0, f))
    w2_spec = pl.BlockSpec(lead + (tf, d), lambda e, f: wsel + esel(e) + (f, 0))
    elead = lead[len(wsel):]
    w13b_spec = pl.BlockSpec(elead + (d, tf), lambda e, f: esel(e) + (0, f))
    w2b_spec = pl.BlockSpec(elead + (tf, d), lambda e, f: esel(e) + (f, 0))
    return pl.pallas_call(
        _s_swiglu_kernel,
        grid=(n_e, ff // tf),
        in_specs=[full, w13_spec, w13_spec, w2_spec, pl.BlockSpec((None, n_rows, 1), lambda e, f: (e, 0, 0)), full],
        out_specs=[full, w13b_spec, w13b_spec, w2b_spec],
        out_shape=[jax.ShapeDtypeStruct((n_rows, d), F32), jax.ShapeDtypeStruct(w1.shape[len(wsel):], BF16),
                   jax.ShapeDtypeStruct(w3.shape[len(wsel):], BF16), jax.ShapeDtypeStruct(w2.shape[len(wsel):], BF16)],
        scratch_shapes=[pltpu.VMEM((n_rows, d), F32)],
        compiler_params=_cparams(2),
        name="s_swiglu",
    )(x, w1, w3, w2, gates, res)


def _s_ple_kernel(x_ref, p_ref, wg_ref, wp_ref, g_ref, gf_ref, o_ref, y_ref):
    x = x_ref[...]
    z = _dot3(_rmsnorm(x, g_ref[...]), wg_ref[...])
    x3 = x + jax.nn.sigmoid(z) * _dot3(p_ref[...], wp_ref[...])
    o_ref[...] = x3
    y_ref[...] = _rmsnorm(x3, gf_ref[...])


def _s_ple(x, p, wg, wp, g, layer, g_final):
    n_rows, d = x.shape
    pd = p.shape[-1]
    full = pl.BlockSpec((n_rows, d), lambda i: (0, 0))
    return pl.pallas_call(
        _s_ple_kernel,
        grid=(1,),
        in_specs=[full,
                  pl.BlockSpec((None, n_rows, pd), lambda i: (layer, 0, 0)),
                  pl.BlockSpec((None, d, d), lambda i: (layer, 0, 0), pipeline_mode=pl.Buffered(1)),
                  pl.BlockSpec((None, pd, d), lambda i: (layer, 0, 0), pipeline_mode=pl.Buffered(1)),
                  pl.BlockSpec((None, 1, d), lambda i: (layer, 0, 0)),
                  pl.BlockSpec((1, d), lambda i: (0, 0))],
        out_specs=[full, full],
        out_shape=[jax.ShapeDtypeStruct((n_rows, d), F32), jax.ShapeDtypeStruct((n_rows, d), F32)],
        compiler_params=_cparams(1),
        name="s_ple",
    )(x, p, wg, wp, g, g_final)


def kernel(x_prompt, x_sample, cache_k, cache_v, state_conv, page_table, p_prompt, p_sample, norm_mix, w_in, conv_w,
           w_out, norm_ffn, dense_w1, dense_w3, dense_w2, moe_router, moe_w1, moe_w3, moe_w2, norm_ple, w_ple_gate,
           w_ple_proj, norm_final):
    b, s, d = x_prompt.shape
    db, t_len, _ = x_sample.shape
    depth = w_in.shape[0]
    n_heads, page, hd = cache_k.shape[2:]
    n_pages = page_table.shape[1]
    n_experts = moe_router.shape[-1]
    ppb = BLOCK // page
    past_len = n_pages * page
    assert n_heads * hd == d and w_in.shape[-1] == N_IN_FIELDS * d and BLOCK % page == 0 and hd == LANE
    assert past_len % BLOCK == 0, "cached tail pages of the own block are not supported"
    assert s % ROW_TILE == 0 and s % page == 0 and t_len <= SUBLANE

    bs, ds = b * s, db * t_len
    tp = SUBLANE

    slopes = jnp.exp2(-8.0 * jnp.arange(1, n_heads + 1, dtype=F32) / n_heads)
    w_out_b = w_out.astype(BF16)
    wg_b, wp_b = w_ple_gate.astype(BF16), w_ple_proj.astype(BF16)
    n_exp_pad = 2 * SUBLANE
    assert n_experts <= n_exp_pad
    router_t = jnp.pad(moe_router.transpose(0, 2, 1), ((0, 0), (0, n_exp_pad - n_experts), (0, 0)))
    r_hi = router_t.astype(BF16)
    r_lo = (router_t - r_hi.astype(F32)).astype(BF16)
    router_cat = jnp.concatenate([r_hi, r_hi, r_lo], axis=-1)
    g_mix, g_ffn, g_ple = (g.reshape(depth, 1, d) for g in (norm_mix, norm_ffn, norm_ple))
    g_final = norm_final.reshape(1, d)

    x = x_prompt.reshape(bs, d)
    xs = x_sample.reshape(ds, d)
    p_dim = p_prompt.shape[-1]
    p_p = p_prompt.reshape(depth, bs, p_dim)
    p_s = p_sample.reshape(depth, ds, p_dim)

    means_t = _block_means(cache_k, page_table).transpose(0, 1, 3, 2, 4)
    state_t = state_conv.transpose(0, 2, 1, 3)
    batch_ix = jnp.arange(db)[:, None, None, None, None]
    page_off = jnp.arange(ppb)

    bm_moe = 3 * ROW_TILE
    bm_dense = _row_tile(bs, 4 * ROW_TILE)
    n_dense_chunks = bs // bm_dense
    dense_tables = (jnp.zeros((n_dense_chunks,), I32), jnp.arange(n_dense_chunks, dtype=I32),
                    jnp.full((n_dense_chunks,), bm_dense, I32))
    one_gate = jnp.ones((1, ds, 1), F32)

    k_all = jnp.zeros((depth, b, s // page, n_heads, page, hd), F32)
    v_all = jnp.zeros_like(k_all)
    cp, ksm, vsm, csm = [], [], [], []
    xn = _norm_rows(x, g_mix, 0)
    ys = None
    for i in range(depth):
        hs = _s_inproj(xs, g_mix, w_in, i)
        qkv = hs[:, :3 * d].reshape(db, t_len, 3, n_heads, hd).transpose(2, 0, 3, 1, 4)
        qkv_p = jnp.pad(qkv, ((0, 0), (0, 0), (0, 0), (0, tp - t_len), (0, 0)))
        picks = _sample_select(qkv_p[0], means_t, i)[:, :, :t_len, :TOPK_BLOCKS]
        picks = picks.transpose(0, 2, 1, 3)
        cols = picks[..., None] * ppb + page_off
        phys = page_table[batch_ix, cols]
        att_s = _sample_attend(qkv_p[0], qkv_p[1], qkv_p[2], cache_k, cache_v, phys.reshape(-1).astype(I32),
                               picks.reshape(-1).astype(I32), slopes, i, t_len, past_len)
        att_s = att_s[:, :, :t_len].transpose(2, 0, 1, 3).reshape(t_len, db, d)
        fields_s = hs.reshape(db, t_len, N_IN_FIELDS, d).transpose(2, 1, 0, 3)
        mix_s, new_state = _mix_sample(fields_s, att_s, state_t[i], conv_w, i)
        mix_s = mix_s.transpose(1, 0, 2).reshape(ds, d)
        if i % 2 == 0:
            x1s, xns = _s_outproj(mix_s, xs, w_out, g_ffn, i)
            x2s, fw1, fw3, fw2 = _s_swiglu(xns, dense_w1, dense_w3, dense_w2, one_gate, x1s, wsel=(i // 2,))
        else:
            x1s, xns, e_s, gate_s = _s_outproj(mix_s, xs, w_out, g_ffn, i, router_t[i // 2], n_experts)
            ex = jnp.arange(n_experts, dtype=I32)[:, None]
            gates = sum(jnp.where(e_s[k][None, :] == ex, gate_s[k][None, :], 0.0) for k in range(MOE_TOPK))
            x2s, fw1, fw3, fw2 = _s_swiglu(xns, moe_w1, moe_w3, moe_w2, gates[:, :, None], x1s, wsel=(i // 2,))
        xs, ys = _s_ple(x2s, p_s, w_ple_gate, w_ple_proj, g_ple, i, g_final)

        h = _inproj(xn, w_in, i)
        att_p, k_all, v_all = _moba_prompt(h, slopes, k_all, v_all, i, b, s, n_heads, hd)
        mix, u_tail = _mix_prompt(h, att_p, conv_w, i, b, s, d)
        if i % 2 == 0:
            x1, xf = _outproj(mix, x, w_out_b, g_ffn, i)
            x2 = _swiglu(xf, fw1, fw3, fw2, x1, *dense_tables, wsel=(), bm=bm_dense, routed=False)
        else:
            x1, xf, e_t, gate_t = _outproj(mix, x, w_out_b, g_ffn, i, router_cat[i // 2], n_experts)
            dest, tok_sorted, gate_sorted, sub_active, chunk_e, chunk_blk, chunk_rows = _route(
                e_t[:MOE_TOPK].T, gate_t[:MOE_TOPK].T, n_experts, bm_moe)
            xg = _gather_rows(xf, tok_sorted, sub_active, d // LANE)
            yg = _swiglu(xg, fw1, fw3, fw2, gate_sorted, chunk_e, chunk_blk, chunk_rows, wsel=(), bm=bm_moe,
                         routed=True)
            x2 = _combine(x1, yg, dest)
        last = i == depth - 1
        x, xn = _ple(x2, p_p, wg_b, wp_b, g_ple, i, g_final if last else g_mix[i + 1], F32 if last else BF16)

        cp.append(u_tail[:, SUBLANE - (CONV_W - 1):, :])
        ksm.append(qkv[1]); vsm.append(qkv[2])
        csm.append(new_state.transpose(1, 0, 2))

    y_prompt = xn.reshape(b, s, d)
    y_sample = ys.reshape(db, t_len, d)
    return (y_prompt, y_sample, k_all, v_all, jnp.stack(cp), jnp.stack(ksm), jnp.stack(vsm), jnp.stack(csm))
```
